```python
import jax, jax.numpy as jnp
from jax import lax
import numpy as np

D_MODEL = 1024
BATCH = 16
SEQ = 4096
DEPTH = 4

CTX_LEN = 256
GRID_W = 64
EPS = 1e-6
NEG_INF = -1e30

HEAD_DIM = 64
ATTN_HEADS = D_MODEL // 128
ATTN_KV_HEADS = ATTN_HEADS // 4
ATTN_GROUP = ATTN_HEADS // ATTN_KV_HEADS
ATTN_WIDTH = ATTN_HEADS * HEAD_DIM
KV_WIDTH = ATTN_KV_HEADS * HEAD_DIM
WINDOW = 128
BLOCK = 128
ROPE_BASE = 10000.0
ROPE_FREQS = HEAD_DIM // 4

SSM_WIDTH = D_MODEL // 2
SSM_GROUP = 16
SSM_GROUPS = SSM_WIDTH // SSM_GROUP
SSM_STATE = 64
DT_MIN = 1e-3
DT_MAX = 1e-1

EVEN_IN = 2 * ATTN_WIDTH + 2 * KV_WIDTH + 2 * SSM_WIDTH
EVEN_MIX = ATTN_WIDTH + SSM_WIDTH

POOL_WIDTH = D_MODEL
POOL_WINDOWS = (2, 4, 8, 16)
POOL_GROUP = POOL_WIDTH // len(POOL_WINDOWS)

kernel_name = 'hybrid_swa_s5_pool_prefix_dit'


def rmsnorm(x, g):
    xf = x.astype(jnp.float32)
    y = xf * lax.rsqrt(jnp.mean(xf * xf, axis=-1, keepdims=True) + EPS)
    return (y * g.astype(jnp.float32)).astype(x.dtype)


def axial_rope_tables(n_tokens):
    rows = n_tokens // GRID_W
    row = jnp.repeat(jnp.arange(rows, dtype=jnp.float32), GRID_W)
    col = jnp.tile(jnp.arange(GRID_W, dtype=jnp.float32), rows)
    inv_freq = ROPE_BASE ** (-jnp.arange(ROPE_FREQS, dtype=jnp.float32) / ROPE_FREQS)
    ang = jnp.stack([row[:, None] * inv_freq, col[:, None] * inv_freq], axis=1)
    ang = ang[:, None]
    return jnp.cos(ang), jnp.sin(ang)


def apply_axial_rope(x, cos, sin):
    b, l, h, _ = x.shape
    xr = x.astype(jnp.float32).reshape(b, l, h, 2, 2, ROPE_FREQS)
    x1, x2 = xr[..., 0, :], xr[..., 1, :]
    out = jnp.stack([x1 * cos - x2 * sin, x2 * cos + x1 * sin], axis=-2)
    return out.reshape(b, l, h, HEAD_DIM).astype(x.dtype)


def window_attention(q, k, v, kc, vc, sink):
    b, l, hkv, g, dh = q.shape
    nb = l // BLOCK
    scale = dh ** -0.5
    qb = q.reshape(b, nb, BLOCK, hkv, g, dh)
    pad = ((0, 0), (BLOCK, BLOCK), (0, 0), (0, 0))
    kp = jnp.pad(k, pad).reshape(b, nb + 2, BLOCK, hkv, dh)
    vp = jnp.pad(v, pad).reshape(b, nb + 2, BLOCK, hkv, dh)
    kb = jnp.concatenate([kp[:, :-2], kp[:, 1:-1], kp[:, 2:]], axis=2)
    vb = jnp.concatenate([vp[:, :-2], vp[:, 1:-1], vp[:, 2:]], axis=2)
    s_loc = jnp.einsum('bnqhgd,bnkhd->bnhgqk', qb, kb).astype(jnp.float32) * scale
    s_ctx = jnp.einsum('bnqhgd,bkhd->bnhgqk', qb, kc).astype(jnp.float32) * scale
    qpos = np.arange(nb)[:, None] * BLOCK + np.arange(BLOCK)[None, :]
    kpos = (np.arange(nb)[:, None] - 1) * BLOCK + np.arange(3 * BLOCK)[None, :]
    valid = ((np.abs(qpos[:, :, None] - kpos[:, None, :]) <= WINDOW)
             & (kpos[:, None, :] >= 0) & (kpos[:, None, :] < l))
    s_loc = jnp.where(valid[None, :, None, None], s_loc, NEG_INF)
    s_sink = jnp.broadcast_to(sink.astype(jnp.float32)[None, None, :, :, None, None],
                              s_loc.shape[:-1] + (1,))
    p = jax.nn.softmax(jnp.concatenate([s_loc, s_ctx, s_sink], axis=-1), axis=-1)
    n_loc = 3 * BLOCK
    n_ctx = kc.shape[1]
    p_loc = p[..., :n_loc].astype(v.dtype)
    p_ctx = p[..., n_loc:n_loc + n_ctx].astype(v.dtype)
    out = (jnp.einsum('bnhgqk,bnkhd->bnqhgd', p_loc, vb)
           + jnp.einsum('bnhgqk,bkhd->bnqhgd', p_ctx, vc))
    return out.reshape(b, l, hkv * g * dh)


def context_attention(qc, kc, vc, sink):
    b, lc, hkv, g, dh = qc.shape
    s = jnp.einsum('bqhgd,bkhd->bhgqk', qc, kc).astype(jnp.float32) * dh ** -0.5
    s_sink = jnp.broadcast_to(sink.astype(jnp.float32)[None, :, :, None, None], s.shape[:-1] + (1,))
    p = jax.nn.softmax(jnp.concatenate([s, s_sink], axis=-1), axis=-1)[..., :lc].astype(vc.dtype)
    return jnp.einsum('bhgqk,bkhd->bqhgd', p, vc).reshape(b, lc, hkv * g * dh)


def s5_discretize(a_re, a_im, log_dt, b_re, b_im):
    lam = lax.complex(a_re.astype(jnp.float32), a_im.astype(jnp.float32))
    dt = jnp.exp(log_dt.astype(jnp.float32))[:, None]
    a_bar = jnp.exp(lam * dt)
    bmat = lax.complex(b_re.astype(jnp.float32), b_im.astype(jnp.float32))
    b_bar = ((a_bar - 1.0) / lam)[..., None] * bmat
    return a_bar, b_bar


def _scan_combine(left, right):
    a_l, h_l = left
    a_r, h_r = right
    return a_l * a_r, a_r * h_l + h_r


def diag_scan(a_bar, bu, h0):
    if h0 is not None:
        bu = bu.at[:, 0].add(a_bar * h0)
    a = jnp.broadcast_to(a_bar, bu.shape)
    _, h = lax.associative_scan(_scan_combine, (a, bu), axis=1)
    return h


def s5_branch(u, uc, a_re, a_im, log_dt, b_re, b_im, c_re, c_im, d_skip, glu_w, glu_b, need_ctx):
    b, l, _ = u.shape
    lc = uc.shape[1]
    ul = u.astype(jnp.float32).reshape(b, l, SSM_GROUPS, SSM_GROUP).astype(jnp.complex64)
    ucg = uc.astype(jnp.float32).reshape(b, lc, SSM_GROUPS, SSM_GROUP).astype(jnp.complex64)
    d = d_skip.astype(jnp.float32)
    y = u.astype(jnp.float32) * d
    yc = uc.astype(jnp.float32) * d if need_ctx else None
    for direction in range(2):
        a_bar, b_bar = s5_discretize(a_re[direction], a_im[direction], log_dt[direction],
                                     b_re[direction], b_im[direction])
        cmat = lax.complex(c_re[direction].astype(jnp.float32), c_im[direction].astype(jnp.float32))
        bu = jnp.einsum('blgc,gpc->blgp', ul, b_bar)
        buc = jnp.einsum('blgc,gpc->blgp', ucg, b_bar)
        if direction == 1:
            bu, buc = bu[:, ::-1], buc[:, ::-1]
        hc = diag_scan(a_bar, buc, None)
        h = diag_scan(a_bar, bu, hc[:, -1])
        if direction == 1:
            h, hc = h[:, ::-1], hc[:, ::-1]
        y = y + jnp.real(jnp.einsum('blgp,gcp->blgc', h, cmat)).reshape(b, l, SSM_WIDTH)
        if need_ctx:
            yc = yc + jnp.real(jnp.einsum('blgp,gcp->blgc', hc, cmat)).reshape(b, lc, SSM_WIDTH)
    gw = glu_w.astype(jnp.float32)
    gb = glu_b.astype(jnp.float32)

    def glu(z):
        z = jax.nn.gelu(z)
        return z * jax.nn.sigmoid(z @ gw + gb)

    out = glu(y).astype(u.dtype)
    out_c = glu(yc).astype(u.dtype) if need_ctx else None
    return out, out_c


def attn_ssm_mixer(a, ac, cos, sin, w_in, w_out, sink, a_re, a_im, log_dt, b_re, b_im,
                   c_re, c_im, d_skip, glu_w, glu_b, need_ctx):
    b, l, _ = a.shape
    lc = ac.shape[1]
    cuts = [int(v) for v in np.cumsum([ATTN_WIDTH, KV_WIDTH, KV_WIDTH, ATTN_WIDTH, SSM_WIDTH])]
    q, k, v, g_attn, u, g_ssm = jnp.split(a @ w_in, cuts, axis=-1)
    qc, kc, vc, g_attn_c, uc, g_ssm_c = jnp.split(ac @ w_in, cuts, axis=-1)
    q = apply_axial_rope(q.reshape(b, l, ATTN_HEADS, HEAD_DIM), cos, sin)
    q = q.reshape(b, l, ATTN_KV_HEADS, ATTN_GROUP, HEAD_DIM)
    k = apply_axial_rope(k.reshape(b, l, ATTN_KV_HEADS, HEAD_DIM), cos, sin)
    v = v.reshape(b, l, ATTN_KV_HEADS, HEAD_DIM)
    kc = kc.reshape(b, lc, ATTN_KV_HEADS, HEAD_DIM)
    vc = vc.reshape(b, lc, ATTN_KV_HEADS, HEAD_DIM)
    sink = sink.reshape(ATTN_KV_HEADS, ATTN_GROUP)
    o_attn = window_attention(q, k, v, kc, vc, sink) * jax.nn.silu(g_attn)
    o_ssm, o_ssm_c = s5_branch(u, uc, a_re, a_im, log_dt, b_re, b_im, c_re, c_im,
                               d_skip, glu_w, glu_b, need_ctx)
    y = jnp.concatenate([o_attn, o_ssm * jax.nn.silu(g_ssm)], axis=-1) @ w_out
    yc = None
    if need_ctx:
        qc = qc.reshape(b, lc, ATTN_KV_HEADS, ATTN_GROUP, HEAD_DIM)
        o_attn_c = context_attention(qc, kc, vc, sink) * jax.nn.silu(g_attn_c)
        yc = jnp.concatenate([o_attn_c, o_ssm_c * jax.nn.silu(g_ssm_c)], axis=-1) @ w_out
    return y, yc


def multiscale_pool(u):
    t_len = u.shape[1]
    uf = u.astype(jnp.float32)
    cs = jnp.pad(jnp.cumsum(uf, axis=1), ((0, 0), (1, 0), (0, 0)))
    pos = np.arange(t_len)
    outs = []
    for gi, w in enumerate(POOL_WINDOWS):
        r = w // 2
        lo = np.clip(pos - r, 0, t_len)
        hi = np.clip(pos + r + 1, 0, t_len)
        inv_cnt = jnp.asarray(1.0 / (hi - lo), dtype=jnp.float32)[None, :, None]
        sl = slice(gi * POOL_GROUP, (gi + 1) * POOL_GROUP)
        csg = cs[..., sl]
        outs.append((csg[:, hi] - csg[:, lo]) * inv_cnt - uf[..., sl])
    return jnp.concatenate(outs, axis=-1)


def pool_mixer(a, w_in, w_out, pool_w, pool_scale):
    u, gate = jnp.split(a @ w_in, 2, axis=-1)
    b, t, _ = u.shape
    p = multiscale_pool(u).reshape(b, t, len(POOL_WINDOWS), POOL_GROUP)
    p = jnp.einsum('btgc,gcd->btgd', p, pool_w.astype(jnp.float32)).reshape(b, t, POOL_WIDTH)
    p = (p * pool_scale.astype(jnp.float32)).astype(a.dtype)
    return (p * jax.nn.silu(gate)) @ w_out


def setup_inputs(seed: int = 0) -> dict:
    key = jax.random.key(seed)
    ks = iter(jax.random.split(key, 32))
    n_even = (DEPTH + 1) // 2
    n_odd = DEPTH // 2

    def nrm(shape, scale):
        return jax.random.normal(next(ks), shape, jnp.float32) * scale

    n_idx = jnp.arange(SSM_STATE, dtype=jnp.float32)
    ssm_shape = (n_even, 2, SSM_GROUPS, SSM_STATE)
    return {
        'x': nrm((BATCH, SEQ, D_MODEL), 1.0),
        'c': nrm((BATCH, D_MODEL), 1.0),
        'ctx': nrm((BATCH, CTX_LEN, D_MODEL), 1.0),
        'c_ctx': nrm((D_MODEL,), 1.0),
        'ada_w': nrm((DEPTH, D_MODEL, 3 * D_MODEL), 0.5 * D_MODEL ** -0.5),
        'ada_b': nrm((DEPTH, 3 * D_MODEL), 0.01),
        'norm_g': 1.0 + nrm((DEPTH, D_MODEL), 0.02),
        'even_w_in': nrm((n_even, D_MODEL, EVEN_IN), D_MODEL ** -0.5),
        'even_w_out': nrm((n_even, EVEN_MIX, D_MODEL), EVEN_MIX ** -0.5),
        'attn_sink': nrm((n_even, ATTN_HEADS), 0.5),
        'ssm_a_re': -0.5 + nrm(ssm_shape, 0.01),
        'ssm_a_im': jnp.pi * n_idx + nrm(ssm_shape, 0.01),
        'ssm_log_dt': jax.random.uniform(next(ks), (n_even, 2, SSM_GROUPS), jnp.float32,
                                         np.log(DT_MIN), np.log(DT_MAX)),
        'ssm_b_re': nrm((n_even, 2, SSM_GROUPS, SSM_STATE, SSM_GROUP), (2 * SSM_GROUP) ** -0.5),
        'ssm_b_im': nrm((n_even, 2, SSM_GROUPS, SSM_STATE, SSM_GROUP), (2 * SSM_GROUP) ** -0.5),
        'ssm_c_re': nrm((n_even, 2, SSM_GROUPS, SSM_GROUP, SSM_STATE), (2 * SSM_STATE) ** -0.5),
        'ssm_c_im': nrm((n_even, 2, SSM_GROUPS, SSM_GROUP, SSM_STATE), (2 * SSM_STATE) ** -0.5),
        'ssm_d': nrm((n_even, SSM_WIDTH), 0.5),
        'glu_w': nrm((n_even, SSM_WIDTH, SSM_WIDTH), SSM_WIDTH ** -0.5),
        'glu_b': nrm((n_even, SSM_WIDTH), 0.01),
        'odd_w_in': nrm((n_odd, D_MODEL, 2 * POOL_WIDTH), D_MODEL ** -0.5),
        'odd_w_out': nrm((n_odd, POOL_WIDTH, D_MODEL), POOL_WIDTH ** -0.5),
        'pool_w': nrm((n_odd, len(POOL_WINDOWS), POOL_GROUP, POOL_GROUP), POOL_GROUP ** -0.5),
        'pool_scale': 1.0 + nrm((n_odd, POOL_WIDTH), 0.02),
        'final_g': 1.0 + nrm((D_MODEL,), 0.02),
    }


def reference(x, c, ctx, c_ctx, ada_w, ada_b, norm_g, even_w_in, even_w_out, attn_sink,
              ssm_a_re, ssm_a_im, ssm_log_dt, ssm_b_re, ssm_b_im, ssm_c_re, ssm_c_im, ssm_d,
              glu_w, glu_b, odd_w_in, odd_w_out, pool_w, pool_scale, final_g):
    cos, sin = axial_rope_tables(x.shape[1])
    h, hc = x, ctx
    s_lat = jax.nn.silu(c)
    s_ctx = jax.nn.silu(c_ctx)
    for i in range(DEPTH):
        need_ctx = i < DEPTH - 1
        shift, scale, gate = jnp.split((s_lat @ ada_w[i] + ada_b[i])[:, None, :], 3, axis=-1)
        shift_c, scale_c, gate_c = jnp.split(s_ctx @ ada_w[i] + ada_b[i], 3, axis=-1)
        a = rmsnorm(h, norm_g[i]) * (1.0 + scale) + shift
        ac = rmsnorm(hc, norm_g[i]) * (1.0 + scale_c) + shift_c
        j = i // 2
        if i % 2 == 0:
            y, yc = attn_ssm_mixer(a, ac, cos, sin, even_w_in[j], even_w_out[j], attn_sink[j],
                                   ssm_a_re[j], ssm_a_im[j], ssm_log_dt[j], ssm_b_re[j], ssm_b_im[j],
                                   ssm_c_re[j], ssm_c_im[j], ssm_d[j], glu_w[j], glu_b[j], need_ctx)
        else:
            y = pool_mixer(a, odd_w_in[j], odd_w_out[j], pool_w[j], pool_scale[j])
            yc = pool_mixer(ac, odd_w_in[j], odd_w_out[j], pool_w[j], pool_scale[j]) if need_ctx else None
        h = h + gate * y
        if need_ctx:
            hc = hc + gate_c * yc
    return rmsnorm(h, final_g)
```

```python
import functools
import math

import numpy as np
import jax
import jax.numpy as jnp
from jax import lax
from jax.experimental import pallas as pl
from jax.experimental.pallas import tpu as pltpu

D_MODEL = 1024
DEPTH = 4
GRID_W = 64
EPS = 1e-6
NEG_INF = -1e30

HEAD_DIM = 64
ATTN_HEADS = 8
KV_HEADS = 2
ATTN_GROUP = ATTN_HEADS // KV_HEADS
ATTN_WIDTH = ATTN_HEADS * HEAD_DIM
KV_WIDTH = KV_HEADS * HEAD_DIM
WINDOW = 128
BLOCK = 128
ROPE_BASE = 10000.0
ROPE_FREQS = HEAD_DIM // 4

SSM_WIDTH = 512
SSM_GROUP = 16
SSM_GROUPS = 32
SSM_STATE = 64
CHUNK = 16
GFLAT = CHUNK * SSM_GROUP
PAIRS = SSM_GROUPS // 2

POOL_WINDOWS = (2, 4, 8, 16)
POOL_GROUP = D_MODEL // len(POOL_WINDOWS)
POOL_HALO = 8

LANES = 128
VMEM_LIMIT = 56 * 1024 * 1024

F32 = jnp.float32
BF16 = jnp.bfloat16


def _sigmoid(x):
    return 1.0 / (1.0 + jnp.exp(-x))


def _silu(x):
    return x * _sigmoid(x)


def _gelu_tanh(x):
    c = math.sqrt(2.0 / math.pi)
    return 0.5 * x * (1.0 + jnp.tanh(c * (x + 0.044715 * (x * x * x))))


def _cparams(sem):
    return pltpu.CompilerParams(dimension_semantics=sem, vmem_limit_bytes=VMEM_LIMIT)


def _ada_kernel(s_ref, w_ref, b_ref, o_ref):
    s = _silu(s_ref[...]).astype(BF16)
    y = jnp.dot(s, w_ref[0].astype(BF16), preferred_element_type=F32)
    o_ref[0] = y + b_ref[0]


def _ada_call(s_all, ada_w, ada_b):
    rows = s_all.shape[0]
    n = ada_w.shape[-1]
    tn = D_MODEL
    return pl.pallas_call(
        _ada_kernel,
        grid=(DEPTH, n // tn),
        in_specs=[
            pl.BlockSpec((rows, D_MODEL), lambda i, j: (0, 0)),
            pl.BlockSpec((1, D_MODEL, tn), lambda i, j: (i, 0, j)),
            pl.BlockSpec((1, 1, tn), lambda i, j: (i, 0, j)),
        ],
        out_specs=pl.BlockSpec((1, rows, tn), lambda i, j: (i, 0, j)),
        out_shape=jax.ShapeDtypeStruct((DEPTH, rows, n), F32),
        compiler_params=_cparams(("arbitrary", "arbitrary")),
        name="ada_mod",
    )(s_all, ada_w, ada_b.reshape(DEPTH, 1, n))


def _rope(y, cos, sin_signed):
    lane = lax.broadcasted_iota(jnp.int32, (1, LANES), 1)
    first_half = (lane % (2 * ROPE_FREQS)) < ROPE_FREQS
    outs = []
    for j in range(y.shape[1] // LANES):
        yj = y[:, j * LANES:(j + 1) * LANES]
        up = pltpu.roll(yj, LANES - ROPE_FREQS, 1)
        dn = pltpu.roll(yj, ROPE_FREQS, 1)
        partner = jnp.where(first_half, up, dn)
        outs.append(yj * cos + partner * sin_signed)
    return jnp.concatenate(outs, axis=1)


def _modproj_kernel(*refs, segs, use_rope):
    if use_rope:
        x_ref, mod_ref, g_ref, w_ref, cos_ref, sin_ref = refs[:6]
        out_refs = refs[6:]
    else:
        x_ref, mod_ref, g_ref, w_ref = refs[:4]
        out_refs = refs[4:]
    x = x_ref[0]
    r = lax.rsqrt(jnp.mean(x * x, axis=-1, keepdims=True) + EPS)
    shift = mod_ref[0, 0:1, :]
    scale = mod_ref[0, 1:2, :]
    a = (x * r * g_ref[...]) * (1.0 + scale) + shift
    ab = a.astype(BF16)
    for (start, width, rope), o_ref in zip(segs, out_refs):
        y = jnp.dot(ab, w_ref[:, start:start + width], preferred_element_type=F32)
        if rope:
            y = _rope(y, cos_ref[...], sin_ref[...])
        o_ref[0] = y.astype(o_ref.dtype)


def _modproj_call(x, mod, g, w, segs, out_dtypes, tm, rope_tabs=None, name="modproj"):
    b, t, d = x.shape
    n = w.shape[1]
    use_rope = rope_tabs is not None
    mod_map = (lambda bi, ti: (bi, 0, 0)) if mod.shape[0] == b else (lambda bi, ti: (0, 0, 0))
    in_specs = [
        pl.BlockSpec((1, tm, d), lambda bi, ti: (bi, ti, 0)),
        pl.BlockSpec((1, 3, d), mod_map),
        pl.BlockSpec((1, d), lambda bi, ti: (0, 0)),
        pl.BlockSpec((d, n), lambda bi, ti: (0, 0)),
    ]
    args = [x, mod, g.reshape(1, d), w]
    if use_rope:
        in_specs += [pl.BlockSpec((tm, LANES), lambda bi, ti: (ti, 0))] * 2
        args += list(rope_tabs)
    out_specs = [pl.BlockSpec((1, tm, wd), lambda bi, ti: (bi, ti, 0)) for (_, wd, _) in segs]
    out_shape = [jax.ShapeDtypeStruct((b, t, wd), dt) for (_, wd, _), dt in zip(segs, out_dtypes)]
    return pl.pallas_call(
        functools.partial(_modproj_kernel, segs=tuple(segs), use_rope=use_rope),
        grid=(b, t // tm),
        in_specs=in_specs,
        out_specs=out_specs,
        out_shape=out_shape,
        compiler_params=_cparams(("parallel", "parallel")),
        name=name,
    )(*args)


def _attn_kernel(sink_ref, q_ref, kc_ref, vc_ref, ga_ref, *rest, tq, local, seq_len):
    if local:
        k_ref, v_ref, o_ref = rest
    else:
        (o_ref,) = rest
    qi = pl.program_id(1)
    pair_w = ATTN_GROUP * HEAD_DIM
    head_of_lane = lax.broadcasted_iota(jnp.int32, (1, pair_w), 1) // HEAD_DIM
    nt_dims = (((1,), (1,)), ((), ()))

    def block(i, carry):
        r0 = pl.multiple_of(i * BLOCK, BLOCK)
        q0 = qi * tq + i * BLOCK
        if local:
            start = pl.multiple_of(jnp.clip(q0 - BLOCK, 0, seq_len - 3 * BLOCK), BLOCK)
            qpos = q0 + lax.broadcasted_iota(jnp.int32, (BLOCK, 3 * BLOCK), 0)
            kpos = start + lax.broadcasted_iota(jnp.int32, (BLOCK, 3 * BLOCK), 1)
            valid1 = jnp.abs(qpos - kpos) <= WINDOW
            valid = jnp.concatenate([valid1] * ATTN_GROUP, axis=0)
        for hk in range(KV_HEADS):
            lanes = slice(hk * pair_w, (hk + 1) * pair_w)
            kv_lanes = slice(hk * 2 * HEAD_DIM, (hk + 1) * 2 * HEAD_DIM)
            qb = q_ref[0, pl.ds(r0, BLOCK), lanes]
            zero = jnp.zeros_like(qb)
            qs = jnp.concatenate(
                [jnp.where(head_of_lane == g, qb, zero) for g in range(ATTN_GROUP)], axis=0)
            sink_col = jnp.concatenate(
                [jnp.full((BLOCK, 1), sink_ref[hk * ATTN_GROUP + g], F32) for g in range(ATTN_GROUP)],
                axis=0)
            kc = kc_ref[0, :, kv_lanes]
            kc4 = jnp.concatenate([kc, kc], axis=1)
            vc = vc_ref[0, :, kv_lanes]
            vc4 = jnp.concatenate([vc, vc], axis=1)
            s_ctx = lax.dot_general(qs, kc4, nt_dims, preferred_element_type=F32)
            m = jnp.maximum(jnp.max(s_ctx, axis=-1, keepdims=True), sink_col)
            if local:
                kl = k_ref[0, pl.ds(start, 3 * BLOCK), kv_lanes]
                kl4 = jnp.concatenate([kl, kl], axis=1)
                vl = v_ref[0, pl.ds(start, 3 * BLOCK), kv_lanes]
                vl4 = jnp.concatenate([vl, vl], axis=1)
                s_loc = lax.dot_general(qs, kl4, nt_dims, preferred_element_type=F32)
                s_loc = jnp.where(valid, s_loc, NEG_INF)
                m = jnp.maximum(m, jnp.max(s_loc, axis=-1, keepdims=True))
            e_ctx = jnp.exp(s_ctx - m)
            den = jnp.sum(e_ctx, axis=-1, keepdims=True) + jnp.exp(sink_col - m)
            o = jnp.dot(e_ctx.astype(BF16), vc4, preferred_element_type=F32)
            if local:
                e_loc = jnp.exp(s_loc - m)
                den = den + jnp.sum(e_loc, axis=-1, keepdims=True)
                o = o + jnp.dot(e_loc.astype(BF16), vl4, preferred_element_type=F32)
            o = o * (1.0 / den)
            oh = jnp.zeros((BLOCK, pair_w), F32)
            for g in range(ATTN_GROUP):
                oh = oh + jnp.where(head_of_lane == g, o[g * BLOCK:(g + 1) * BLOCK, :], 0.0)
            gate = ga_ref[0, pl.ds(r0, BLOCK), lanes].astype(F32)
            o_ref[0, pl.ds(r0, BLOCK), lanes] = (oh * _silu(gate)).astype(o_ref.dtype)
        return carry

    lax.fori_loop(0, tq // BLOCK, block, 0)


def _attn_call(sink, q, kc4, vc4, ga, k4=None, v4=None, tq=512, name="attn"):
    b, t, _ = q.shape
    lc = kc4.shape[1]
    local = k4 is not None
    in_specs = [
        pl.BlockSpec(memory_space=pltpu.SMEM),
        pl.BlockSpec((1, tq, ATTN_WIDTH), lambda bi, ti: (bi, ti, 0)),
        pl.BlockSpec((1, lc, 2 * KV_WIDTH), lambda bi, ti: (bi, 0, 0)),
        pl.BlockSpec((1, lc, 2 * KV_WIDTH), lambda bi, ti: (bi, 0, 0)),
        pl.BlockSpec((1, tq, ATTN_WIDTH), lambda bi, ti: (bi, ti, 0)),
    ]
    args = [sink, q, kc4, vc4, ga]
    if local:
        in_specs += [pl.BlockSpec((1, t, 2 * KV_WIDTH), lambda bi, ti: (bi, 0, 0))] * 2
        args += [k4, v4]
    return pl.pallas_call(
        functools.partial(_attn_kernel, tq=tq, local=local, seq_len=t),
        grid=(b, t // tq),
        in_specs=in_specs,
        out_specs=pl.BlockSpec((1, tq, ATTN_WIDTH), lambda bi, ti: (bi, ti, 0)),
        out_shape=jax.ShapeDtypeStruct((b, t, ATTN_WIDTH), BF16),
        compiler_params=_cparams(("parallel", "arbitrary")),
        name=name,
    )(*args)


S5_TILE = 512
CROWS = 16


def _s5_kernel(zc_ref, zl_ref, m_ref, s_ref, o_ref, a_ref, d_ref, yc_ref, yl_ref,
               hc_buf, hl_buf, s_buf):
    half = 2 * SSM_STATE

    def scan_tile(z_ref, h_buf, row0, nrows, direction, carry):
        cols = slice(direction * 2 * half, (direction + 1) * 2 * half)
        s_buf[0:nrows, :] = jnp.dot(z_ref[pl.ds(row0, nrows), :], s_ref[0, :, cols],
                                    preferred_element_type=F32)
        a_re = a_ref[0, 2 * direction:2 * direction + 1, :]
        a_im = a_ref[0, 2 * direction + 1:2 * direction + 2, :]
        nchunks = nrows // CROWS

        def step(i, c):
            h_re, h_im = c
            j = i if direction == 0 else nchunks - 1 - i
            lr = pl.multiple_of(j * CROWS, CROWS)
            gr = pl.multiple_of(row0 + j * CROWS, CROWS)
            h_buf[pl.ds(gr, CROWS), cols] = jnp.concatenate([h_re, h_im], axis=1).astype(BF16)
            s_re = s_buf[pl.ds(lr, CROWS), 0:half]
            s_im = s_buf[pl.ds(lr, CROWS), half:2 * half]
            n_re = a_re * h_re - a_im * h_im + s_re
            n_im = a_re * h_im + a_im * h_re + s_im
            return n_re, n_im

        return lax.fori_loop(0, nchunks, step, carry)

    nc_rows = zc_ref.shape[0]
    nl_rows = zl_ref.shape[0]
    zero = jnp.zeros((CROWS, half), F32)
    carry = scan_tile(zc_ref, hc_buf, 0, nc_rows, 0, (zero, zero))
    for t in range(nl_rows // S5_TILE):
        carry = scan_tile(zl_ref, hl_buf, t * S5_TILE, S5_TILE, 0, carry)
    carry = scan_tile(zc_ref, hc_buf, 0, nc_rows, 1, (zero, zero))
    for t in reversed(range(nl_rows // S5_TILE)):
        carry = scan_tile(zl_ref, hl_buf, t * S5_TILE, S5_TILE, 1, carry)

    def emit(z_ref, h_buf, y_ref, row0, nrows):
        z = z_ref[pl.ds(row0, nrows), :]
        y = jnp.dot(z, m_ref[0], preferred_element_type=F32)
        y = y + jnp.dot(h_buf[pl.ds(row0, nrows), :], o_ref[0], preferred_element_type=F32)
        y = y + z.astype(F32) * d_ref[0]
        y_ref[pl.ds(row0, nrows), :] = y.astype(y_ref.dtype)

    emit(zc_ref, hc_buf, yc_ref, 0, nc_rows)
    for t in range(nl_rows // S5_TILE):
        emit(zl_ref, hl_buf, yl_ref, t * S5_TILE, S5_TILE)


def _s5_call(zc, zl, m_op, s_op, o_op, a_op, d_op):
    rc, rl = zc.shape[0], zl.shape[0]
    pw = 2 * GFLAT
    row_spec = lambda r: pl.BlockSpec((r, pw), lambda p: (0, p))
    op_spec = pl.BlockSpec((1, pw, pw), lambda p: (p, 0, 0))
    return pl.pallas_call(
        _s5_kernel,
        grid=(PAIRS,),
        in_specs=[
            row_spec(rc), row_spec(rl), op_spec, op_spec, op_spec,
            pl.BlockSpec((1, 4, 2 * SSM_STATE), lambda p: (p, 0, 0)),
            pl.BlockSpec((1, 1, pw), lambda p: (p, 0, 0)),
        ],
        out_specs=[row_spec(rc), row_spec(rl)],
        out_shape=[jax.ShapeDtypeStruct(zc.shape, F32), jax.ShapeDtypeStruct(zl.shape, F32)],
        scratch_shapes=[
            pltpu.VMEM((rc, pw), BF16),
            pltpu.VMEM((rl, pw), BF16),
            pltpu.VMEM((S5_TILE, 2 * 2 * SSM_STATE), F32),
        ],
        compiler_params=_cparams(("parallel",)),
        name="s5_scan",
    )(zc, zl, m_op, s_op, o_op, a_op, d_op)


def _s5_operators(a_re, a_im, log_dt, b_re, b_im, c_re, c_im, d_skip):
    t = CHUNK
    taus = jnp.arange(t + 1, dtype=F32)
    m_tot = jnp.zeros((SSM_GROUPS, t, SSM_GROUP, t, SSM_GROUP), F32)
    s_ops, o_ops, a_ops = [], [], []
    tri = (np.arange(t)[None, :] - np.arange(t)[:, None])
    for direction in range(2):
        lam = lax.complex(a_re[direction].astype(F32), a_im[direction].astype(F32))
        dt = jnp.exp(log_dt[direction].astype(F32))[:, None]
        a_bar = jnp.exp(lam * dt)
        bmat = lax.complex(b_re[direction].astype(F32), b_im[direction].astype(F32))
        b_bar = ((a_bar - 1.0) / lam)[..., None] * bmat
        cmat = lax.complex(c_re[direction].astype(F32), c_im[direction].astype(F32))
        a_pow = jnp.exp((lam * dt)[None] * taus[:, None, None].astype(jnp.complex64))
        kern = jnp.real(jnp.sum(cmat[None, :, :, None, :] * a_pow[:t, :, None, None, :]
                                * jnp.swapaxes(b_bar, 1, 2)[None, :, None, :, :], axis=-1))
        lag = tri if direction == 0 else -tri
        kfull = kern[np.clip(lag, 0, t - 1)]
        kfull = jnp.where((lag >= 0)[:, :, None, None, None], kfull, 0.0)
        m_tot = m_tot + jnp.transpose(kfull, (2, 0, 4, 1, 3))
        expo = (t - 1 - np.arange(t)) if direction == 0 else np.arange(t)
        s_c = a_pow[expo][:, :, :, None] * b_bar[None]
        s_c = jnp.transpose(s_c, (1, 0, 3, 2))
        s_ops.append((jnp.real(s_c), jnp.imag(s_c)))
        expo_o = (np.arange(t) + 1) if direction == 0 else (t - np.arange(t))
        o_c = cmat[None] * a_pow[expo_o][:, :, None, :]
        o_c = jnp.transpose(o_c, (1, 3, 0, 2))
        o_ops.append((jnp.real(o_c), -jnp.imag(o_c)))
        a_ops.append((jnp.real(a_pow[t]), jnp.imag(a_pow[t])))

    g2 = PAIRS
    m_g = m_tot.reshape(SSM_GROUPS, GFLAT, GFLAT)
    m_pair = jnp.zeros((g2, 2, GFLAT, 2, GFLAT), F32)
    m_pair = m_pair.at[:, 0, :, 0, :].set(m_g[0::2]).at[:, 1, :, 1, :].set(m_g[1::2])
    m_pair = m_pair.reshape(g2, 2 * GFLAT, 2 * GFLAT)

    s_pair = jnp.zeros((g2, 2, GFLAT, 2, 2, 2, SSM_STATE), F32)
    o_pair = jnp.zeros((g2, 2, 2, 2, SSM_STATE, 2, GFLAT), F32)
    for direction in range(2):
        for plane in range(2):
            sv = s_ops[direction][plane].reshape(SSM_GROUPS, GFLAT, SSM_STATE)
            ov = o_ops[direction][plane].reshape(SSM_GROUPS, SSM_STATE, GFLAT)
            for gi in range(2):
                s_pair = s_pair.at[:, gi, :, direction, plane, gi, :].set(sv[gi::2])
                o_pair = o_pair.at[:, direction, plane, gi, :, gi, :].set(ov[gi::2])
    s_pair = s_pair.reshape(g2, 2 * GFLAT, 8 * SSM_STATE)
    o_pair = o_pair.reshape(g2, 8 * SSM_STATE, 2 * GFLAT)

    a_rows = []
    for direction in range(2):
        for plane in range(2):
            av = a_ops[direction][plane]
            a_rows.append(jnp.concatenate([av[0::2], av[1::2]], axis=-1))
    a_pair = jnp.stack(a_rows, axis=1)

    d_g = d_skip.astype(F32).reshape(SSM_GROUPS, 1, SSM_GROUP)
    d_flat = jnp.broadcast_to(d_g, (SSM_GROUPS, CHUNK, SSM_GROUP)).reshape(g2, 1, 2 * GFLAT)
    return m_pair.astype(BF16), s_pair.astype(BF16), o_pair.astype(BF16), a_pair, d_flat


def _to_flat(u):
    b, t, _ = u.shape
    z = u.reshape(b, t // CHUNK, CHUNK, SSM_GROUPS, SSM_GROUP)
    z = jnp.transpose(z, (1, 0, 3, 2, 4))
    return z.reshape((t // CHUNK) * b, SSM_GROUPS * GFLAT)


def _from_flat(y, b):
    r = y.shape[0]
    k = r // b
    z = y.reshape(k, b, SSM_GROUPS, CHUNK, SSM_GROUP)
    z = jnp.transpose(z, (1, 0, 3, 2, 4))
    return z.reshape(b, k * CHUNK, SSM_WIDTH)


def _even_out_kernel(og_ref, y_ref, gs_ref, h_ref, mod_ref, gw_ref, gb_ref, wo_ref, o_ref):
    z = _gelu_tanh(y_ref[0])
    t = jnp.dot(z.astype(BF16), gw_ref[...], preferred_element_type=F32) + gb_ref[...]
    os_ = z * _sigmoid(t) * _silu(gs_ref[0].astype(F32))
    yo = jnp.dot(og_ref[0], wo_ref[0:ATTN_WIDTH, :], preferred_element_type=F32)
    yo = yo + jnp.dot(os_.astype(BF16), wo_ref[ATTN_WIDTH:, :], preferred_element_type=F32)
    o_ref[0] = h_ref[0] + mod_ref[0, 2:3, :] * yo


def _even_out_call(og, y, gs, h, mod, gw, gb, wo, tm, name="even_out"):
    b, t, d = h.shape
    mod_map = (lambda bi, ti: (bi, 0, 0)) if mod.shape[0] == b else (lambda bi, ti: (0, 0, 0))
    tok = lambda w: pl.BlockSpec((1, tm, w), lambda bi, ti: (bi, ti, 0))
    full = lambda r, c: pl.BlockSpec((r, c), lambda bi, ti: (0, 0))
    return pl.pallas_call(
        _even_out_kernel,
        grid=(b, t // tm),
        in_specs=[tok(ATTN_WIDTH), tok(SSM_WIDTH), tok(SSM_WIDTH), tok(d),
                  pl.BlockSpec((1, 3, d), mod_map),
                  full(SSM_WIDTH, SSM_WIDTH), full(1, SSM_WIDTH), full(d, d)],
        out_specs=tok(d),
        out_shape=jax.ShapeDtypeStruct((b, t, d), F32),
        compiler_params=_cparams(("parallel", "parallel")),
        name=name,
    )(og, y, gs, h, mod, gw, gb.reshape(1, SSM_WIDTH), wo)


def _pool_out_kernel(u_ref, up_ref, un_ref, gate_ref, h_ref, mod_ref, pw_ref, ps_ref, wo_ref,
                     fg_ref, o_ref, ext_ref, *, tm, seq_len, final):
    ti = pl.program_id(1)
    nt = pl.num_programs(1)
    prev = jnp.where(ti > 0, up_ref[0], 0.0)
    nxt = jnp.where(ti < nt - 1, un_ref[0], 0.0)
    ext_ref[0:POOL_HALO, :] = prev
    ext_ref[POOL_HALO:POOL_HALO + tm, :] = u_ref[0]
    ext_ref[POOL_HALO + tm:, :] = nxt
    pos = ti * tm + lax.broadcasted_iota(jnp.int32, (tm, 1), 0)
    parts = []
    for gi, w in enumerate(POOL_WINDOWS):
        r = w // 2
        lanes = slice(gi * POOL_GROUP, (gi + 1) * POOL_GROUP)
        acc = ext_ref[POOL_HALO - r:POOL_HALO - r + tm, lanes]
        for dlt in range(-r + 1, r + 1):
            acc = acc + ext_ref[POOL_HALO + dlt:POOL_HALO + dlt + tm, lanes]
        cnt = (jnp.minimum(pos + r + 1, seq_len) - jnp.maximum(pos - r, 0)).astype(F32)
        p = acc * (1.0 / cnt) - u_ref[0, :, lanes]
        pm = jnp.dot(p.astype(BF16), pw_ref[gi], preferred_element_type=F32)
        parts.append(pm)
    p_all = jnp.concatenate(parts, axis=1) * ps_ref[...]
    mixed = p_all * _silu(gate_ref[0].astype(F32))
    yo = jnp.dot(mixed.astype(BF16), wo_ref[...], preferred_element_type=F32)
    hn = h_ref[0] + mod_ref[0, 2:3, :] * yo
    if final:
        rr = lax.rsqrt(jnp.mean(hn * hn, axis=-1, keepdims=True) + EPS)
        hn = hn * rr * fg_ref[...]
    o_ref[0] = hn


def _pool_out_call(u, gate, h, mod, pool_w, pool_scale, wo, final_g, tm, final, name="pool_out"):
    b, t, d = h.shape
    hb = tm // POOL_HALO
    nhalo = t // POOL_HALO
    mod_map = (lambda bi, ti: (bi, 0, 0)) if mod.shape[0] == b else (lambda bi, ti: (0, 0, 0))
    tok = lambda: pl.BlockSpec((1, tm, d), lambda bi, ti: (bi, ti, 0))
    return pl.pallas_call(
        functools.partial(_pool_out_kernel, tm=tm, seq_len=t, final=final),
        grid=(b, t // tm),
        in_specs=[
            tok(),
            pl.BlockSpec((1, POOL_HALO, d), lambda bi, ti: (bi, jnp.maximum(ti * hb - 1, 0), 0)),
            pl.BlockSpec((1, POOL_HALO, d), lambda bi, ti: (bi, jnp.minimum((ti + 1) * hb, nhalo - 1), 0)),
            tok(), tok(),
            pl.BlockSpec((1, 3, d), mod_map),
            pl.BlockSpec((len(POOL_WINDOWS), POOL_GROUP, POOL_GROUP), lambda bi, ti: (0, 0, 0)),
            pl.BlockSpec((1, d), lambda bi, ti: (0, 0)),
            pl.BlockSpec((d, d), lambda bi, ti: (0, 0)),
            pl.BlockSpec((1, d), lambda bi, ti: (0, 0)),
        ],
        out_specs=tok(),
        out_shape=jax.ShapeDtypeStruct((b, t, d), F32),
        scratch_shapes=[pltpu.VMEM((tm + 2 * POOL_HALO, d), F32)],
        compiler_params=_cparams(("parallel", "arbitrary")),
        name=name,
    )(u, u, u, gate, h, mod, pool_w, pool_scale.reshape(1, d), wo, final_g.reshape(1, d))


def _rope_tables(n_tokens):
    pos = np.arange(n_tokens)
    row = (pos // GRID_W).astype(np.float32)
    col = (pos % GRID_W).astype(np.float32)
    inv_freq = ROPE_BASE ** (-jnp.arange(ROPE_FREQS, dtype=F32) / ROPE_FREQS)
    ang_r = jnp.asarray(row)[:, None] * inv_freq
    ang_c = jnp.asarray(col)[:, None] * inv_freq
    cos_h = jnp.concatenate([jnp.cos(ang_r)] * 2 + [jnp.cos(ang_c)] * 2, axis=1)
    sin_h = jnp.concatenate([-jnp.sin(ang_r), jnp.sin(ang_r), -jnp.sin(ang_c), jnp.sin(ang_c)], axis=1)
    return jnp.tile(cos_h, (1, 2)), jnp.tile(sin_h, (1, 2))


def _even_weights(w_in):
    c = np.cumsum([0, ATTN_WIDTH, KV_WIDTH, KV_WIDTH, ATTN_WIDTH, SSM_WIDTH, SSM_WIDTH])
    wq = w_in[:, c[0]:c[1]] * (HEAD_DIM ** -0.5)
    wk = w_in[:, c[1]:c[2]]
    wv = w_in[:, c[2]:c[3]]
    dup = lambda w: jnp.concatenate([w[:, :HEAD_DIM]] * 2 + [w[:, HEAD_DIM:]] * 2, axis=1)
    w = jnp.concatenate([wq, dup(wk), dup(wv), w_in[:, c[3]:]], axis=1)
    return w.astype(BF16)


EVEN_SEGS_LAT = ((0, 512, True), (512, 256, True), (768, 256, False), (1024, 512, False),
                 (1536, 512, False), (2048, 512, False))
EVEN_SEGS_CTX = tuple((s, w, False) for (s, w, _) in EVEN_SEGS_LAT)
EVEN_DTYPES = (BF16,) * 6
ODD_SEGS = ((0, D_MODEL, False), (D_MODEL, D_MODEL, False))
ODD_DTYPES = (F32, BF16)


def kernel(x, c, ctx, c_ctx, ada_w, ada_b, norm_g, even_w_in, even_w_out, attn_sink,
           ssm_a_re, ssm_a_im, ssm_log_dt, ssm_b_re, ssm_b_im, ssm_c_re, ssm_c_im, ssm_d,
           glu_w, glu_b, odd_w_in, odd_w_out, pool_w, pool_scale, final_g):
    b, l, d = x.shape
    lc = ctx.shape[1]
    tm = 512
    cos_t, sin_t = _rope_tables(l)

    s_all = jnp.zeros((24, d), F32).at[:b].set(c).at[b].set(c_ctx)
    mods = _ada_call(s_all, ada_w, ada_b).reshape(DEPTH, 24, 3, d)
    h, hc = x, ctx
    for i in range(DEPTH):
        need_ctx = i < DEPTH - 1
        mod = mods[i, :b]
        mod_c = mods[i, b:b + 1]
        j = i // 2
        if i % 2 == 0:
            w_in = _even_weights(even_w_in[j])
            wo = even_w_out[j].astype(BF16)
            gw = glu_w[j].astype(BF16)
            sink = attn_sink[j].astype(F32)
            q, k4, v4, ga, u, gs = _modproj_call(h, mod, norm_g[i], w_in, EVEN_SEGS_LAT, EVEN_DTYPES,
                                                 tm, rope_tabs=(cos_t, sin_t), name="even_in")
            qc, kc4, vc4, gac, uc, gsc = _modproj_call(hc, mod_c, norm_g[i], w_in, EVEN_SEGS_CTX,
                                                       EVEN_DTYPES, lc, name="even_in_ctx")
            og = _attn_call(sink, q, kc4, vc4, ga, k4, v4, tq=512, name="attn")
            ops = _s5_operators(ssm_a_re[j], ssm_a_im[j], ssm_log_dt[j], ssm_b_re[j], ssm_b_im[j],
                                ssm_c_re[j], ssm_c_im[j], ssm_d[j])
            yc_f, yl_f = _s5_call(_to_flat(uc), _to_flat(u), *ops)
            y = _from_flat(yl_f, b)
            h_new = _even_out_call(og, y, gs, h, mod, gw, glu_b[j], wo, tm, name="even_out")
            if need_ctx:
                ogc = _attn_call(sink, qc, kc4, vc4, gac, tq=lc, name="attn_ctx")
                yc = _from_flat(yc_f, b)
                hc = _even_out_call(ogc, yc, gsc, hc, mod_c, gw, glu_b[j], wo, lc, name="even_out_ctx")
            h = h_new
        else:
            w_in = odd_w_in[j].astype(BF16)
            wo = odd_w_out[j].astype(BF16)
            pw = pool_w[j].astype(BF16)
            final = i == DEPTH - 1
            u, gate = _modproj_call(h, mod, norm_g[i], w_in, ODD_SEGS, ODD_DTYPES, tm, name="odd_in")
            h_new = _pool_out_call(u, gate, h, mod, pw, pool_scale[j], wo, final_g, tm, final,
                                   name="pool_out")
            if need_ctx:
                ucx, gatec = _modproj_call(hc, mod_c, norm_g[i], w_in, ODD_SEGS, ODD_DTYPES, lc,
                                           name="odd_in_ctx")
                hc = _pool_out_call(ucx, gatec, hc, mod_c, pw, pool_scale[j], wo, final_g, lc, False,
                                    name="pool_out_ctx")
            h = h_new
    return h
```

```python
import functools
import math

import numpy as np
import jax
import jax.numpy as jnp
from jax import lax
from jax.experimental import pallas as pl
from jax.experimental.pallas import tpu as pltpu

D_MODEL = 1024
DEPTH = 4
GRID_W = 64
EPS = 1e-6
NEG_INF = -1e30
LOG2E = math.log2(math.e)

HEAD_DIM = 64
ATTN_HEADS = 8
KV_HEADS = 2
ATTN_GROUP = ATTN_HEADS // KV_HEADS
ATTN_WIDTH = ATTN_HEADS * HEAD_DIM
KV_WIDTH = KV_HEADS * HEAD_DIM
WINDOW = 128
BLOCK = 128
ROPE_BASE = 10000.0
ROPE_FREQS = HEAD_DIM // 4

SSM_WIDTH = 512
SSM_GROUP = 16
SSM_GROUPS = 32
SSM_STATE = 64
CHUNK = 16
GFLAT = CHUNK * SSM_GROUP
PAIRS = SSM_GROUPS // 2
FLAT_W = SSM_GROUPS * GFLAT

POOL_WINDOWS = (2, 4, 8, 16)
POOL_GROUP = D_MODEL // len(POOL_WINDOWS)
POOL_HALO = 8

LANES = 128
PIECES = LANES // SSM_GROUP
VMEM_LIMIT = 56 * 1024 * 1024
TOK_BLK = 32

F32 = jnp.float32
BF16 = jnp.bfloat16


def _sigmoid(x):
    return 1.0 / (1.0 + jnp.exp(-x))


def _silu(x):
    return x * _sigmoid(x)


def _gelu_tanh(x):
    c = math.sqrt(2.0 / math.pi)
    return 0.5 * x * (1.0 + jnp.tanh(c * (x + 0.044715 * (x * x * x))))


def _cparams(sem):
    return pltpu.CompilerParams(dimension_semantics=sem, vmem_limit_bytes=VMEM_LIMIT)


def _ada_kernel(s_ref, w_ref, b_ref, o_ref):
    s = _silu(s_ref[...]).astype(BF16)
    y = jnp.dot(s, w_ref[0].astype(BF16), preferred_element_type=F32)
    o_ref[0] = y + b_ref[0]


def _ada_call(s_all, ada_w, ada_b):
    rows = s_all.shape[0]
    n = ada_w.shape[-1]
    tn = D_MODEL
    return pl.pallas_call(
        _ada_kernel,
        grid=(DEPTH, n // tn),
        in_specs=[
            pl.BlockSpec((rows, D_MODEL), lambda i, j: (0, 0)),
            pl.BlockSpec((1, D_MODEL, tn), lambda i, j: (i, 0, j)),
            pl.BlockSpec((1, 1, tn), lambda i, j: (i, 0, j)),
        ],
        out_specs=pl.BlockSpec((1, rows, tn), lambda i, j: (i, 0, j)),
        out_shape=jax.ShapeDtypeStruct((DEPTH, rows, n), F32),
        compiler_params=_cparams(("arbitrary", "arbitrary")),
        name="ada_mod",
    )(s_all, ada_w, ada_b.reshape(DEPTH, 1, n))


def _rope(y, cos, sin_signed):
    lane = lax.broadcasted_iota(jnp.int32, (1, LANES), 1)
    first_half = (lane % (2 * ROPE_FREQS)) < ROPE_FREQS
    outs = []
    for j in range(y.shape[1] // LANES):
        yj = y[:, j * LANES:(j + 1) * LANES]
        up = pltpu.roll(yj, LANES - ROPE_FREQS, 1)
        dn = pltpu.roll(yj, ROPE_FREQS, 1)
        partner = jnp.where(first_half, up, dn)
        outs.append(yj * cos + partner * sin_signed)
    return jnp.concatenate(outs, axis=1)


def _piece_transpose(v):
    piece = lax.broadcasted_iota(jnp.int32, (1, LANES), 1) // SSM_GROUP
    v = list(v)
    for dist in (4, 2, 1):
        keep = (piece & dist) == 0
        nv = list(v)
        for i in range(PIECES):
            if i & dist == 0:
                a, b = v[i], v[i + dist]
                nv[i] = jnp.where(keep, a, pltpu.roll(b, dist * SSM_GROUP, 1))
                nv[i + dist] = jnp.where(keep, pltpu.roll(a, LANES - dist * SSM_GROUP, 1), b)
        v = nv
    return v


def _norm_mod(x_ref, mod_ref, g_ref):
    x = x_ref[...]
    r = lax.rsqrt(jnp.mean(x * x, axis=-1, keepdims=True) + EPS)
    shift = mod_ref[:, 0:1, :]
    scale = mod_ref[:, 1:2, :]
    return (x * r * g_ref[...]) * (1.0 + scale) + shift


EV_Q, EV_K, EV_V, EV_GA, EV_U, EV_GS = 0, 512, 768, 1024, 1536, 2048
EV_N = 2560


def _even_in_kernel(*refs, use_rope):
    if use_rope:
        x_ref, mod_ref, g_ref, w_ref, cos_ref, sin_ref = refs[:6]
        refs = refs[6:]
    else:
        x_ref, mod_ref, g_ref, w_ref = refs[:4]
        refs = refs[4:]
    q_ref, k_ref, v_ref, ga_ref, z_ref, gs_ref, u_scr = refs
    nb, tt, d = x_ref.shape
    rows = nb * tt
    ab = _norm_mod(x_ref, mod_ref, g_ref).reshape(rows, d).astype(BF16)

    def proj(start, width):
        return jnp.dot(ab, w_ref[:, start:start + width], preferred_element_type=F32)

    def put(o_ref, y):
        o_ref[...] = y.astype(o_ref.dtype).reshape(o_ref.shape)

    q = proj(EV_Q, 512)
    k = proj(EV_K, 256)
    if use_rope:
        cos = jnp.concatenate([cos_ref[...]] * nb, axis=0)
        sin = jnp.concatenate([sin_ref[...]] * nb, axis=0)
        q = _rope(q, cos, sin)
        k = _rope(k, cos, sin)
    put(q_ref, q * (HEAD_DIM ** -0.5 * LOG2E))
    put(k_ref, k)
    put(v_ref, proj(EV_V, 256))
    put(ga_ref, proj(EV_GA, 512))
    put(gs_ref, proj(EV_GS, 512))

    u = proj(EV_U, 512)
    for j in range(SSM_WIDTH // LANES):
        u_scr[j] = u[:, j * LANES:(j + 1) * LANES]
    for kk in range(tt // CHUNK):
        for j in range(SSM_WIDTH // LANES):
            for hh in range(CHUNK // PIECES):
                w = _piece_transpose(
                    [u_scr[j, pl.ds(kk * CHUNK + hh * PIECES + i, nb, stride=tt), :]
                     for i in range(PIECES)])
                for qi in range(PIECES):
                    g = j * PIECES + qi
                    col = g * GFLAT + hh * LANES
                    z_ref[kk * nb:(kk + 1) * nb, col:col + LANES] = w[qi].astype(z_ref.dtype)


def _even_in_call(x, mod, g, w, rope_tabs=None, name="even_in"):
    b, t, d = x.shape
    tt = TOK_BLK
    use_rope = rope_tabs is not None
    bm = mod.shape[0]
    tok = lambda wd: pl.BlockSpec((b, tt, wd), lambda ti: (0, ti, 0))
    in_specs = [
        tok(d),
        pl.BlockSpec((bm, 3, d), lambda ti: (0, 0, 0)),
        pl.BlockSpec((1, d), lambda ti: (0, 0)),
        pl.BlockSpec((d, EV_N), lambda ti: (0, 0)),
    ]
    args = [x, mod, g.reshape(1, d), w]
    if use_rope:
        in_specs += [pl.BlockSpec((tt, LANES), lambda ti: (ti, 0))] * 2
        args += list(rope_tabs)
    flat_rows = (tt // CHUNK) * b
    widths = (512, 256, 256, 512, None, 512)
    out_specs, out_shape = [], []
    for wd in widths:
        if wd is None:
            out_specs.append(pl.BlockSpec((flat_rows, FLAT_W), lambda ti: (ti, 0)))
            out_shape.append(jax.ShapeDtypeStruct(((t // CHUNK) * b, FLAT_W), BF16))
        else:
            out_specs.append(tok(wd))
            out_shape.append(jax.ShapeDtypeStruct((b, t, wd), BF16))
    return pl.pallas_call(
        functools.partial(_even_in_kernel, use_rope=use_rope),
        grid=(t // tt,),
        in_specs=in_specs,
        out_specs=out_specs,
        out_shape=out_shape,
        scratch_shapes=[pltpu.VMEM((SSM_WIDTH // LANES, b * tt, LANES), F32)],
        compiler_params=_cparams(("parallel",)),
        name=name,
    )(*args)


def _attn_kernel(sink_ref, q_ref, kc_ref, vc_ref, ga_ref, *rest, tq, local, seq_len):
    if local:
        bias_ref, k_ref, v_ref, o_ref = rest
    else:
        (o_ref,) = rest
    qi = pl.program_id(1)
    pair_w = ATTN_GROUP * HEAD_DIM
    head_of_lane = lax.broadcasted_iota(jnp.int32, (1, pair_w), 1) // HEAD_DIM
    nt_dims = (((1,), (1,)), ((), ()))

    def block(i, carry):
        r0 = pl.multiple_of(i * BLOCK, BLOCK)
        q0 = qi * tq + i * BLOCK
        if local:
            start = pl.multiple_of(jnp.clip(q0 - BLOCK, 0, seq_len - 3 * BLOCK), BLOCK)
            variant = jnp.where(q0 == 0, 0, jnp.where(q0 == seq_len - BLOCK, 2, 1))
            bias1 = bias_ref[variant]
            bias = jnp.concatenate([bias1] * ATTN_GROUP, axis=0)
        for hk in range(KV_HEADS):
            lanes = slice(hk * pair_w, (hk + 1) * pair_w)
            kv_lanes = slice(hk * 2 * HEAD_DIM, (hk + 1) * 2 * HEAD_DIM)
            qb = q_ref[0, pl.ds(r0, BLOCK), lanes]
            zero = jnp.zeros_like(qb)
            qs = jnp.concatenate(
                [jnp.where(head_of_lane == g, qb, zero) for g in range(ATTN_GROUP)], axis=0)
            sink_col = jnp.concatenate(
                [jnp.full((BLOCK, 1), sink_ref[hk * ATTN_GROUP + g] * LOG2E, F32)
                 for g in range(ATTN_GROUP)], axis=0)
            kc = kc_ref[0, :, kv_lanes]
            kc4 = jnp.concatenate([kc, kc], axis=1)
            vc = vc_ref[0, :, kv_lanes]
            vc4 = jnp.concatenate([vc, vc], axis=1)
            s_ctx = lax.dot_general(qs, kc4, nt_dims, preferred_element_type=F32)
            m = jnp.maximum(jnp.max(s_ctx, axis=-1, keepdims=True), sink_col)
            if local:
                kl = k_ref[0, pl.ds(start, 3 * BLOCK), kv_lanes]
                kl4 = jnp.concatenate([kl, kl], axis=1)
                vl = v_ref[0, pl.ds(start, 3 * BLOCK), kv_lanes]
                vl4 = jnp.concatenate([vl, vl], axis=1)
                s_loc = lax.dot_general(qs, kl4, nt_dims, preferred_element_type=F32) + bias
                m = jnp.maximum(m, jnp.max(s_loc, axis=-1, keepdims=True))
            e_ctx = jnp.exp2(s_ctx - m)
            den = jnp.sum(e_ctx, axis=-1, keepdims=True) + jnp.exp2(sink_col - m)
            o = jnp.dot(e_ctx.astype(BF16), vc4, preferred_element_type=F32)
            if local:
                e_loc = jnp.exp2(s_loc - m)
                den = den + jnp.sum(e_loc, axis=-1, keepdims=True)
                o = o + jnp.dot(e_loc.astype(BF16), vl4, preferred_element_type=F32)
            o = o * (1.0 / den)
            oh = jnp.zeros((BLOCK, pair_w), F32)
            for g in range(ATTN_GROUP):
                oh = oh + jnp.where(head_of_lane == g, o[g * BLOCK:(g + 1) * BLOCK, :], 0.0)
            gate = ga_ref[0, pl.ds(r0, BLOCK), lanes].astype(F32)
            o_ref[0, pl.ds(r0, BLOCK), lanes] = (oh * _silu(gate)).astype(o_ref.dtype)
        return carry

    lax.fori_loop(0, tq // BLOCK, block, 0)


def _band_bias():
    i = np.arange(BLOCK)[:, None]
    j = np.arange(3 * BLOCK)[None, :]
    tabs = []
    for off in (0, -BLOCK, -2 * BLOCK):
        valid = np.abs(i - (j + off)) <= WINDOW
        tabs.append(np.where(valid, 0.0, NEG_INF))
    return jnp.asarray(np.stack(tabs), F32)


def _attn_call(sink, q, kc4, vc4, ga, k4=None, v4=None, tq=512, name="attn"):
    b, t, _ = q.shape
    lc = kc4.shape[1]
    local = k4 is not None
    in_specs = [
        pl.BlockSpec(memory_space=pltpu.SMEM),
        pl.BlockSpec((1, tq, ATTN_WIDTH), lambda bi, ti: (bi, ti, 0)),
        pl.BlockSpec((1, lc, 2 * KV_WIDTH), lambda bi, ti: (bi, 0, 0)),
        pl.BlockSpec((1, lc, 2 * KV_WIDTH), lambda bi, ti: (bi, 0, 0)),
        pl.BlockSpec((1, tq, ATTN_WIDTH), lambda bi, ti: (bi, ti, 0)),
    ]
    args = [sink, q, kc4, vc4, ga]
    if local:
        in_specs += [pl.BlockSpec((3, BLOCK, 3 * BLOCK), lambda bi, ti: (0, 0, 0))]
        in_specs += [pl.BlockSpec((1, t, 2 * KV_WIDTH), lambda bi, ti: (bi, 0, 0))] * 2
        args += [_band_bias(), k4, v4]
    return pl.pallas_call(
        functools.partial(_attn_kernel, tq=tq, local=local, seq_len=t),
        grid=(b, t // tq),
        in_specs=in_specs,
        out_specs=pl.BlockSpec((1, tq, ATTN_WIDTH), lambda bi, ti: (bi, ti, 0)),
        out_shape=jax.ShapeDtypeStruct((b, t, ATTN_WIDTH), BF16),
        compiler_params=_cparams(("parallel", "arbitrary")),
        name=name,
    )(*args)


S5_TILE = 512
CROWS = 16


def _s5_kernel(zc_ref, zl_ref, m_ref, s_ref, o_ref, a_ref, d_ref, yc_ref, yl_ref,
               hc_buf, hl_buf, s_buf):
    half = 2 * SSM_STATE

    def scan_tile(z_ref, h_buf, row0, nrows, direction, carry):
        cols = slice(direction * 2 * half, (direction + 1) * 2 * half)
        s_buf[0:nrows, :] = jnp.dot(z_ref[pl.ds(row0, nrows), :], s_ref[0, :, cols],
                                    preferred_element_type=F32)
        a_re = a_ref[0, 2 * direction:2 * direction + 1, :]
        a_im = a_ref[0, 2 * direction + 1:2 * direction + 2, :]
        nchunks = nrows // CROWS

        def step(i, c):
            h_re, h_im = c
            j = i if direction == 0 else nchunks - 1 - i
            lr = pl.multiple_of(j * CROWS, CROWS)
            gr = pl.multiple_of(row0 + j * CROWS, CROWS)
            h_buf[pl.ds(gr, CROWS), cols] = jnp.concatenate([h_re, h_im], axis=1).astype(BF16)
            s_re = s_buf[pl.ds(lr, CROWS), 0:half]
            s_im = s_buf[pl.ds(lr, CROWS), half:2 * half]
            n_re = a_re * h_re - a_im * h_im + s_re
            n_im = a_re * h_im + a_im * h_re + s_im
            return n_re, n_im

        return lax.fori_loop(0, nchunks, step, carry)

    nc_rows = zc_ref.shape[0]
    nl_rows = zl_ref.shape[0]
    zero = jnp.zeros((CROWS, half), F32)
    carry = scan_tile(zc_ref, hc_buf, 0, nc_rows, 0, (zero, zero))
    for t in range(nl_rows // S5_TILE):
        carry = scan_tile(zl_ref, hl_buf, t * S5_TILE, S5_TILE, 0, carry)
    carry = scan_tile(zc_ref, hc_buf, 0, nc_rows, 1, (zero, zero))
    for t in reversed(range(nl_rows // S5_TILE)):
        carry = scan_tile(zl_ref, hl_buf, t * S5_TILE, S5_TILE, 1, carry)

    def emit(z_ref, h_buf, y_ref, row0, nrows):
        z = z_ref[pl.ds(row0, nrows), :]
        y = jnp.dot(z, m_ref[0], preferred_element_type=F32)
        y = y + jnp.dot(h_buf[pl.ds(row0, nrows), :], o_ref[0], preferred_element_type=F32)
        y = y + z.astype(F32) * d_ref[0]
        y_ref[pl.ds(row0, nrows), :] = y.astype(y_ref.dtype)

    emit(zc_ref, hc_buf, yc_ref, 0, nc_rows)
    for t in range(nl_rows // S5_TILE):
        emit(zl_ref, hl_buf, yl_ref, t * S5_TILE, S5_TILE)


def _s5_call(zc, zl, m_op, s_op, o_op, a_op, d_op):
    rc, rl = zc.shape[0], zl.shape[0]
    pw = 2 * GFLAT
    row_spec = lambda r: pl.BlockSpec((r, pw), lambda p: (0, p))
    op_spec = pl.BlockSpec((1, pw, pw), lambda p: (p, 0, 0))
    return pl.pallas_call(
        _s5_kernel,
        grid=(PAIRS,),
        in_specs=[
            row_spec(rc), row_spec(rl), op_spec, op_spec, op_spec,
            pl.BlockSpec((1, 4, 2 * SSM_STATE), lambda p: (p, 0, 0)),
            pl.BlockSpec((1, 1, pw), lambda p: (p, 0, 0)),
        ],
        out_specs=[row_spec(rc), row_spec(rl)],
        out_shape=[jax.ShapeDtypeStruct(zc.shape, F32), jax.ShapeDtypeStruct(zl.shape, F32)],
        scratch_shapes=[
            pltpu.VMEM((rc, pw), BF16),
            pltpu.VMEM((rl, pw), BF16),
            pltpu.VMEM((S5_TILE, 2 * 2 * SSM_STATE), F32),
        ],
        compiler_params=_cparams(("parallel",)),
        name="s5_scan",
    )(zc, zl, m_op, s_op, o_op, a_op, d_op)


def _s5_operators(a_re, a_im, log_dt, b_re, b_im, c_re, c_im, d_skip):
    t = CHUNK
    ne = a_re.shape[0]
    ar, ai = a_re.astype(F32), a_im.astype(F32)
    dt = jnp.exp(log_dt.astype(F32))[..., None]
    taus = jnp.arange(t + 1, dtype=F32).reshape(t + 1, 1, 1, 1, 1)
    mag = jnp.exp(ar * dt * taus)
    ang = ai * dt * taus
    apr, api = mag * jnp.cos(ang), mag * jnp.sin(ang)
    xr, xi = apr[1] - 1.0, api[1]
    den = ar * ar + ai * ai
    cr, ci = (xr * ar + xi * ai) / den, (xi * ar - xr * ai) / den
    br, bi = b_re.astype(F32), b_im.astype(F32)
    bbr = cr[..., None] * br - ci[..., None] * bi
    bbi = cr[..., None] * bi + ci[..., None] * br
    ccr, cci = c_re.astype(F32), c_im.astype(F32)
    wr = ccr * apr[:, :, :, :, None, :] - cci * api[:, :, :, :, None, :]
    wi = ccr * api[:, :, :, :, None, :] + cci * apr[:, :, :, :, None, :]
    hp = lax.Precision.HIGHEST
    kern = (jnp.einsum('tndgcp,ndgpe->tndgce', wr[:t], bbr, precision=hp)
            - jnp.einsum('tndgcp,ndgpe->tndgce', wi[:t], bbi, precision=hp))

    tri = np.arange(t)[None, :] - np.arange(t)[:, None]
    m_tot = 0.0
    for direction in range(2):
        lag = tri if direction == 0 else -tri
        kfull = kern[:, :, direction][np.clip(lag, 0, t - 1)]
        kfull = jnp.where((lag >= 0)[:, :, None, None, None, None], kfull, 0.0)
        m_tot = m_tot + jnp.transpose(kfull, (2, 3, 0, 5, 1, 4))
    eye = jnp.eye(2, dtype=F32)
    m_g = m_tot.reshape(ne, PAIRS, 2, GFLAT, 1, GFLAT)
    m_pair = (m_g * eye.reshape(1, 1, 2, 1, 2, 1)).reshape(ne, PAIRS, 2 * GFLAT, 2 * GFLAT)

    s_planes, o_planes, a_planes = [], [], []
    for direction in range(2):
        expo = (t - 1 - np.arange(t)) if direction == 0 else np.arange(t)
        pr = apr[expo][:, :, direction][..., None]
        pi = api[expo][:, :, direction][..., None]
        qr, qi = bbr[:, direction][None], bbi[:, direction][None]
        s_r, s_i = pr * qr - pi * qi, pr * qi + pi * qr
        s_planes.append([jnp.transpose(v, (1, 2, 0, 4, 3)).reshape(ne, PAIRS, 2, GFLAT, SSM_STATE)
                         for v in (s_r, s_i)])
        expo_o = (np.arange(t) + 1) if direction == 0 else (t - np.arange(t))
        o_r = wr[expo_o][:, :, direction]
        o_i = -wi[expo_o][:, :, direction]
        o_planes.append([jnp.transpose(v, (1, 2, 4, 0, 3)).reshape(ne, PAIRS, 2, SSM_STATE, GFLAT)
                         for v in (o_r, o_i)])
        a_planes.append([v[t][:, direction].reshape(ne, PAIRS, 2 * SSM_STATE) for v in (apr, api)])

    s_all = jnp.stack([jnp.stack(pl_, axis=0) for pl_ in s_planes], axis=0)
    s_all = jnp.transpose(s_all, (2, 3, 4, 5, 0, 1, 6))[..., None, :]
    s_pair = (s_all * eye.reshape(1, 1, 2, 1, 1, 1, 2, 1)).reshape(ne, PAIRS, 2 * GFLAT, 8 * SSM_STATE)
    o_all = jnp.stack([jnp.stack(pl_, axis=0) for pl_ in o_planes], axis=0)
    o_all = jnp.transpose(o_all, (2, 3, 0, 1, 4, 5, 6))[..., None, :]
    o_pair = (o_all * eye.reshape(1, 1, 1, 1, 2, 1, 2, 1)).reshape(ne, PAIRS, 8 * SSM_STATE, 2 * GFLAT)
    a_pair = jnp.stack([a_planes[0][0], a_planes[0][1], a_planes[1][0], a_planes[1][1]], axis=2)

    d_g = d_skip.astype(F32).reshape(ne, PAIRS, 2, 1, SSM_GROUP)
    d_flat = jnp.broadcast_to(d_g, (ne, PAIRS, 2, CHUNK, SSM_GROUP)).reshape(ne, PAIRS, 1, 2 * GFLAT)
    return m_pair.astype(BF16), s_pair.astype(BF16), o_pair.astype(BF16), a_pair, d_flat


def _even_out_kernel(og_ref, yf_ref, gs_ref, h_ref, mod_ref, gw_ref, gb_ref, wo_ref, o_ref, y_scr):
    nb, tt, d = h_ref.shape
    rows = nb * tt
    for kk in range(tt // CHUNK):
        for j in range(SSM_WIDTH // LANES):
            for hh in range(CHUNK // PIECES):
                v = []
                for qi in range(PIECES):
                    col = (j * PIECES + qi) * GFLAT + hh * LANES
                    v.append(yf_ref[kk * nb:(kk + 1) * nb, col:col + LANES])
                w = _piece_transpose(v)
                for i in range(PIECES):
                    tok = kk * CHUNK + hh * PIECES + i
                    y_scr[j, pl.ds(tok, nb, stride=tt), :] = w[i]
    y = jnp.concatenate([y_scr[j] for j in range(SSM_WIDTH // LANES)], axis=1)
    z = _gelu_tanh(y)
    t = jnp.dot(z.astype(BF16), gw_ref[...], preferred_element_type=F32) + gb_ref[...]
    gs = gs_ref[...].reshape(rows, SSM_WIDTH).astype(F32)
    os_ = z * _sigmoid(t) * _silu(gs)
    og = og_ref[...].reshape(rows, ATTN_WIDTH)
    yo = jnp.dot(og, wo_ref[0:ATTN_WIDTH, :], preferred_element_type=F32)
    yo = yo + jnp.dot(os_.astype(BF16), wo_ref[ATTN_WIDTH:, :], preferred_element_type=F32)
    o_ref[...] = h_ref[...] + mod_ref[:, 2:3, :] * yo.reshape(nb, tt, d)


def _even_out_call(og, yflat, gs, h, mod, gw, gb, wo, name="even_out"):
    b, t, d = h.shape
    tt = TOK_BLK
    bm = mod.shape[0]
    tok = lambda w: pl.BlockSpec((b, tt, w), lambda ti: (0, ti, 0))
    full = lambda r, c: pl.BlockSpec((r, c), lambda ti: (0, 0))
    return pl.pallas_call(
        _even_out_kernel,
        grid=(t // tt,),
        in_specs=[tok(ATTN_WIDTH),
                  pl.BlockSpec(((tt // CHUNK) * b, FLAT_W), lambda ti: (ti, 0)),
                  tok(SSM_WIDTH), tok(d),
                  pl.BlockSpec((bm, 3, d), lambda ti: (0, 0, 0)),
                  full(SSM_WIDTH, SSM_WIDTH), full(1, SSM_WIDTH), full(d, d)],
        out_specs=tok(d),
        out_shape=jax.ShapeDtypeStruct((b, t, d), F32),
        scratch_shapes=[pltpu.VMEM((SSM_WIDTH // LANES, b * tt, LANES), F32)],
        compiler_params=_cparams(("parallel",)),
        name=name,
    )(og, yflat, gs, h, mod, gw, gb.reshape(1, SSM_WIDTH), wo)


def _modproj_kernel(x_ref, mod_ref, g_ref, w_ref, *out_refs, segs):
    ab = _norm_mod(x_ref, mod_ref, g_ref)[0].astype(BF16)
    for (start, width), o_ref in zip(segs, out_refs):
        y = jnp.dot(ab, w_ref[:, start:start + width], preferred_element_type=F32)
        o_ref[0] = y.astype(o_ref.dtype)


def _modproj_call(x, mod, g, w, segs, out_dtypes, tm, name="modproj"):
    b, t, d = x.shape
    n = w.shape[1]
    mod_map = (lambda bi, ti: (bi, 0, 0)) if mod.shape[0] == b else (lambda bi, ti: (0, 0, 0))
    in_specs = [
        pl.BlockSpec((1, tm, d), lambda bi, ti: (bi, ti, 0)),
        pl.BlockSpec((1, 3, d), mod_map),
        pl.BlockSpec((1, d), lambda bi, ti: (0, 0)),
        pl.BlockSpec((d, n), lambda bi, ti: (0, 0)),
    ]
    out_specs = [pl.BlockSpec((1, tm, wd), lambda bi, ti: (bi, ti, 0)) for (_, wd) in segs]
    out_shape = [jax.ShapeDtypeStruct((b, t, wd), dt) for (_, wd), dt in zip(segs, out_dtypes)]
    return pl.pallas_call(
        functools.partial(_modproj_kernel, segs=tuple(segs)),
        grid=(b, t // tm),
        in_specs=in_specs,
        out_specs=out_specs,
        out_shape=out_shape,
        compiler_params=_cparams(("parallel", "parallel")),
        name=name,
    )(x, mod, g.reshape(1, d), w)


def _pool_out_kernel(u_ref, up_ref, un_ref, gate_ref, h_ref, mod_ref, pw_ref, ps_ref, wo_ref,
                     fg_ref, o_ref, ext_ref, *, tm, seq_len, final):
    ti = pl.program_id(1)
    nt = pl.num_programs(1)
    prev = jnp.where(ti > 0, up_ref[0], 0.0)
    nxt = jnp.where(ti < nt - 1, un_ref[0], 0.0)
    ext_ref[0:POOL_HALO, :] = prev
    ext_ref[POOL_HALO:POOL_HALO + tm, :] = u_ref[0]
    ext_ref[POOL_HALO + tm:, :] = nxt
    pos = ti * tm + lax.broadcasted_iota(jnp.int32, (tm, 1), 0)
    parts = []
    for gi, w in enumerate(POOL_WINDOWS):
        r = w // 2
        lanes = slice(gi * POOL_GROUP, (gi + 1) * POOL_GROUP)
        acc = ext_ref[POOL_HALO - r:POOL_HALO - r + tm, lanes]
        for dlt in range(-r + 1, r + 1):
            acc = acc + ext_ref[POOL_HALO + dlt:POOL_HALO + dlt + tm, lanes]
        cnt = (jnp.minimum(pos + r + 1, seq_len) - jnp.maximum(pos - r, 0)).astype(F32)
        p = acc * (1.0 / cnt) - u_ref[0, :, lanes]
        pm = jnp.dot(p.astype(BF16), pw_ref[gi], preferred_element_type=F32)
        parts.append(pm)
    p_all = jnp.concatenate(parts, axis=1) * ps_ref[...]
    mixed = p_all * _silu(gate_ref[0].astype(F32))
    yo = jnp.dot(mixed.astype(BF16), wo_ref[...], preferred_element_type=F32)
    hn = h_ref[0] + mod_ref[0, 2:3, :] * yo
    if final:
        rr = lax.rsqrt(jnp.mean(hn * hn, axis=-1, keepdims=True) + EPS)
        hn = hn * rr * fg_ref[...]
    o_ref[0] = hn


def _pool_out_call(u, gate, h, mod, pool_w, pool_scale, wo, final_g, tm, final, name="pool_out"):
    b, t, d = h.shape
    hb = tm // POOL_HALO
    nhalo = t // POOL_HALO
    mod_map = (lambda bi, ti: (bi, 0, 0)) if mod.shape[0] == b else (lambda bi, ti: (0, 0, 0))
    tok = lambda: pl.BlockSpec((1, tm, d), lambda bi, ti: (bi, ti, 0))
    return pl.pallas_call(
        functools.partial(_pool_out_kernel, tm=tm, seq_len=t, final=final),
        grid=(b, t // tm),
        in_specs=[
            tok(),
            pl.BlockSpec((1, POOL_HALO, d), lambda bi, ti: (bi, jnp.maximum(ti * hb - 1, 0), 0)),
            pl.BlockSpec((1, POOL_HALO, d), lambda bi, ti: (bi, jnp.minimum((ti + 1) * hb, nhalo - 1), 0)),
            tok(), tok(),
            pl.BlockSpec((1, 3, d), mod_map),
            pl.BlockSpec((len(POOL_WINDOWS), POOL_GROUP, POOL_GROUP), lambda bi, ti: (0, 0, 0)),
            pl.BlockSpec((1, d), lambda bi, ti: (0, 0)),
            pl.BlockSpec((d, d), lambda bi, ti: (0, 0)),
            pl.BlockSpec((1, d), lambda bi, ti: (0, 0)),
        ],
        out_specs=tok(),
        out_shape=jax.ShapeDtypeStruct((b, t, d), F32),
        scratch_shapes=[pltpu.VMEM((tm + 2 * POOL_HALO, d), F32)],
        compiler_params=_cparams(("parallel", "arbitrary")),
        name=name,
    )(u, u, u, gate, h, mod, pool_w, pool_scale.reshape(1, d), wo, final_g.reshape(1, d))


def _rope_tables(n_tokens):
    pos = np.arange(n_tokens)
    row = (pos // GRID_W).astype(np.float32)
    col = (pos % GRID_W).astype(np.float32)
    inv_freq = ROPE_BASE ** (-jnp.arange(ROPE_FREQS, dtype=F32) / ROPE_FREQS)
    ang_r = jnp.asarray(row)[:, None] * inv_freq
    ang_c = jnp.asarray(col)[:, None] * inv_freq
    cos_h = jnp.concatenate([jnp.cos(ang_r)] * 2 + [jnp.cos(ang_c)] * 2, axis=1)
    sin_h = jnp.concatenate([-jnp.sin(ang_r), jnp.sin(ang_r), -jnp.sin(ang_c), jnp.sin(ang_c)], axis=1)
    return jnp.tile(cos_h, (1, 2)), jnp.tile(sin_h, (1, 2))


def _even_weights(w_in):
    c = np.cumsum([0, ATTN_WIDTH, KV_WIDTH, KV_WIDTH, ATTN_WIDTH, SSM_WIDTH, SSM_WIDTH])
    wq = w_in[:, c[0]:c[1]]
    wk = w_in[:, c[1]:c[2]]
    wv = w_in[:, c[2]:c[3]]
    dup = lambda w: jnp.concatenate([w[:, :HEAD_DIM]] * 2 + [w[:, HEAD_DIM:]] * 2, axis=1)
    w = jnp.concatenate([wq, dup(wk), dup(wv), w_in[:, c[3]:]], axis=1)
    return w.astype(BF16)


ODD_SEGS = ((0, D_MODEL), (D_MODEL, D_MODEL))
ODD_DTYPES = (F32, BF16)


def kernel(x, c, ctx, c_ctx, ada_w, ada_b, norm_g, even_w_in, even_w_out, attn_sink,
           ssm_a_re, ssm_a_im, ssm_log_dt, ssm_b_re, ssm_b_im, ssm_c_re, ssm_c_im, ssm_d,
           glu_w, glu_b, odd_w_in, odd_w_out, pool_w, pool_scale, final_g):
    b, l, d = x.shape
    lc = ctx.shape[1]
    tm = 512
    rope_tabs = _rope_tables(l)

    s_all = jnp.zeros((24, d), F32).at[:b].set(c).at[b].set(c_ctx)
    mods = _ada_call(s_all, ada_w, ada_b).reshape(DEPTH, 24, 3, d)
    s5_ops = _s5_operators(ssm_a_re, ssm_a_im, ssm_log_dt, ssm_b_re, ssm_b_im,
                           ssm_c_re, ssm_c_im, ssm_d)
    h, hc = x, ctx
    for i in range(DEPTH):
        need_ctx = i < DEPTH - 1
        mod = mods[i, :b]
        mod_c = mods[i, b:b + 1]
        j = i // 2
        if i % 2 == 0:
            w_in = _even_weights(even_w_in[j])
            wo = even_w_out[j].astype(BF16)
            gw = glu_w[j].astype(BF16)
            sink = attn_sink[j].astype(F32)
            q, k4, v4, ga, zl, gs = _even_in_call(h, mod, norm_g[i], w_in, rope_tabs, name="even_in")
            qc, kc4, vc4, gac, zc, gsc = _even_in_call(hc, mod_c, norm_g[i], w_in, None,
                                                       name="even_in_ctx")
            og = _attn_call(sink, q, kc4, vc4, ga, k4, v4, tq=512, name="attn")
            yc_f, yl_f = _s5_call(zc, zl, *[op[j] for op in s5_ops])
            h_new = _even_out_call(og, yl_f, gs, h, mod, gw, glu_b[j], wo, name="even_out")
            if need_ctx:
                ogc = _attn_call(sink, qc, kc4, vc4, gac, tq=lc, name="attn_ctx")
                hc = _even_out_call(ogc, yc_f, gsc, hc, mod_c, gw, glu_b[j], wo, name="even_out_ctx")
            h = h_new
        else:
            w_in = odd_w_in[j].astype(BF16)
            wo = odd_w_out[j].astype(BF16)
            pw = pool_w[j].astype(BF16)
            final = i == DEPTH - 1
            u, gate = _modproj_call(h, mod, norm_g[i], w_in, ODD_SEGS, ODD_DTYPES, tm, name="odd_in")
            h_new = _pool_out_call(u, gate, h, mod, pw, pool_scale[j], wo, final_g, tm, final,
                                   name="pool_out")
            if need_ctx:
                ucx, gatec = _modproj_call(hc, mod_c, norm_g[i], w_in, ODD_SEGS, ODD_DTYPES, lc,
                                           name="odd_in_ctx")
                hc = _pool_out_call(ucx, gatec, hc, mod_c, pw, pool_scale[j], wo, final_g, lc, False,
                                    name="pool_out_ctx")
            h = h_new
    return h
```

```python
import functools
import math

import numpy as np
import jax
import jax.numpy as jnp
from jax import lax
from jax.experimental import pallas as pl
from jax.experimental.pallas import tpu as pltpu

D_MODEL = 1024
DEPTH = 4
GRID_W = 64
EPS = 1e-6
NEG_INF = -1e30
LOG2E = math.log2(math.e)

HEAD_DIM = 64
ATTN_HEADS = 8
KV_HEADS = 2
ATTN_GROUP = ATTN_HEADS // KV_HEADS
ATTN_WIDTH = ATTN_HEADS * HEAD_DIM
KV_WIDTH = KV_HEADS * HEAD_DIM
WINDOW = 128
BLOCK = 128
ROPE_BASE = 10000.0
ROPE_FREQS = HEAD_DIM // 4

SSM_WIDTH = 512
SSM_GROUP = 16
SSM_GROUPS = 32
SSM_STATE = 64
CHUNK = 16
GFLAT = CHUNK * SSM_GROUP
PAIRS = SSM_GROUPS // 2
FLAT_W = SSM_GROUPS * GFLAT

POOL_WINDOWS = (2, 4, 8, 16)
POOL_GROUP = D_MODEL // len(POOL_WINDOWS)
POOL_HALO = 8

LANES = 128
PIECES = LANES // SSM_GROUP
VMEM_LIMIT = 56 * 1024 * 1024
ATTN_TQ = 4096
TOK_BLK = 32

F32 = jnp.float32
BF16 = jnp.bfloat16


def _sigmoid(x):
    return 1.0 / (1.0 + jnp.exp(-x))


def _silu(x):
    return x * _sigmoid(x)


def _gelu_tanh(x):
    c = math.sqrt(2.0 / math.pi)
    return 0.5 * x * (1.0 + jnp.tanh(c * (x + 0.044715 * (x * x * x))))


def _cparams(sem):
    return pltpu.CompilerParams(dimension_semantics=sem, vmem_limit_bytes=VMEM_LIMIT)


def _ada_kernel(s_ref, w_ref, b_ref, o_ref):
    s = _silu(s_ref[...]).astype(BF16)
    y = jnp.dot(s, w_ref[0].astype(BF16), preferred_element_type=F32)
    o_ref[0] = y + b_ref[0]


def _ada_call(s_all, ada_w, ada_b):
    rows = s_all.shape[0]
    n = ada_w.shape[-1]
    tn = D_MODEL
    return pl.pallas_call(
        _ada_kernel,
        grid=(DEPTH, n // tn),
        in_specs=[
            pl.BlockSpec((rows, D_MODEL), lambda i, j: (0, 0)),
            pl.BlockSpec((1, D_MODEL, tn), lambda i, j: (i, 0, j)),
            pl.BlockSpec((1, 1, tn), lambda i, j: (i, 0, j)),
        ],
        out_specs=pl.BlockSpec((1, rows, tn), lambda i, j: (i, 0, j)),
        out_shape=jax.ShapeDtypeStruct((DEPTH, rows, n), F32),
        compiler_params=_cparams(("arbitrary", "arbitrary")),
        name="ada_mod",
    )(s_all, ada_w, ada_b.reshape(DEPTH, 1, n))


def _rope(y, cos, sin_signed):
    lane = lax.broadcasted_iota(jnp.int32, (1, LANES), 1)
    first_half = (lane % (2 * ROPE_FREQS)) < ROPE_FREQS
    outs = []
    for j in range(y.shape[1] // LANES):
        yj = y[:, j * LANES:(j + 1) * LANES]
        up = pltpu.roll(yj, LANES - ROPE_FREQS, 1)
        dn = pltpu.roll(yj, ROPE_FREQS, 1)
        partner = jnp.where(first_half, up, dn)
        outs.append(yj * cos + partner * sin_signed)
    return jnp.concatenate(outs, axis=1)


def _piece_transpose(v):
    piece = lax.broadcasted_iota(jnp.int32, (1, LANES), 1) // SSM_GROUP
    v = list(v)
    for dist in (4, 2, 1):
        keep = (piece & dist) == 0
        nv = list(v)
        for i in range(PIECES):
            if i & dist == 0:
                a, b = v[i], v[i + dist]
                nv[i] = jnp.where(keep, a, pltpu.roll(b, dist * SSM_GROUP, 1))
                nv[i + dist] = jnp.where(keep, pltpu.roll(a, LANES - dist * SSM_GROUP, 1), b)
        v = nv
    return v


def _norm_mod(x_ref, mod_ref, g_ref):
    x = x_ref[...]
    r = lax.rsqrt(jnp.mean(x * x, axis=-1, keepdims=True) + EPS)
    shift = mod_ref[:, 0:1, :]
    scale = mod_ref[:, 1:2, :]
    return (x * r * g_ref[...]) * (1.0 + scale) + shift


EV_Q, EV_K, EV_V, EV_GA, EV_U, EV_GS = 0, 512, 768, 1024, 1536, 2048
EV_N = 2560


def _even_in_kernel(*refs, use_rope):
    if use_rope:
        x_ref, mod_ref, g_ref, w_ref, cos_ref, sin_ref = refs[:6]
        refs = refs[6:]
    else:
        x_ref, mod_ref, g_ref, w_ref = refs[:4]
        refs = refs[4:]
    q_ref, k_ref, v_ref, ga_ref, z_ref, gs_ref, u_scr = refs
    nb, tt, d = x_ref.shape
    rows = nb * tt
    ab = _norm_mod(x_ref, mod_ref, g_ref).reshape(rows, d).astype(BF16)

    def proj(start, width):
        return jnp.dot(ab, w_ref[:, start:start + width], preferred_element_type=F32)

    def put(o_ref, y):
        o_ref[...] = y.astype(o_ref.dtype).reshape(o_ref.shape)

    q = proj(EV_Q, 512)
    k = proj(EV_K, 256)
    if use_rope:
        cos = jnp.concatenate([cos_ref[...]] * nb, axis=0)
        sin = jnp.concatenate([sin_ref[...]] * nb, axis=0)
        q = _rope(q, cos, sin)
        k = _rope(k, cos, sin)
    put(q_ref, q * (HEAD_DIM ** -0.5 * LOG2E))
    put(k_ref, k)
    put(v_ref, proj(EV_V, 256))
    put(ga_ref, proj(EV_GA, 512))
    put(gs_ref, proj(EV_GS, 512))

    u = proj(EV_U, 512)
    for j in range(SSM_WIDTH // LANES):
        u_scr[j] = u[:, j * LANES:(j + 1) * LANES]
    for kk in range(tt // CHUNK):
        for j in range(SSM_WIDTH // LANES):
            for hh in range(CHUNK // PIECES):
                w = _piece_transpose(
                    [u_scr[j, pl.ds(kk * CHUNK + hh * PIECES + i, nb, stride=tt), :]
                     for i in range(PIECES)])
                for qi in range(PIECES):
                    g = j * PIECES + qi
                    col = g * GFLAT + hh * LANES
                    z_ref[kk * nb:(kk + 1) * nb, col:col + LANES] = w[qi].astype(z_ref.dtype)


def _even_in_call(x, mod, g, w, rope_tabs=None, name="even_in"):
    b, t, d = x.shape
    tt = TOK_BLK
    use_rope = rope_tabs is not None
    bm = mod.shape[0]
    tok = lambda wd: pl.BlockSpec((b, tt, wd), lambda ti: (0, ti, 0))
    in_specs = [
        tok(d),
        pl.BlockSpec((bm, 3, d), lambda ti: (0, 0, 0)),
        pl.BlockSpec((1, d), lambda ti: (0, 0)),
        pl.BlockSpec((d, EV_N), lambda ti: (0, 0)),
    ]
    args = [x, mod, g.reshape(1, d), w]
    if use_rope:
        in_specs += [pl.BlockSpec((tt, LANES), lambda ti: (ti, 0))] * 2
        args += list(rope_tabs)
    flat_rows = (tt // CHUNK) * b
    widths = (512, 256, 256, 512, None, 512)
    out_specs, out_shape = [], []
    for wd in widths:
        if wd is None:
            out_specs.append(pl.BlockSpec((flat_rows, FLAT_W), lambda ti: (ti, 0)))
            out_shape.append(jax.ShapeDtypeStruct(((t // CHUNK) * b, FLAT_W), BF16))
        else:
            out_specs.append(tok(wd))
            out_shape.append(jax.ShapeDtypeStruct((b, t, wd), BF16))
    return pl.pallas_call(
        functools.partial(_even_in_kernel, use_rope=use_rope),
        grid=(t // tt,),
        in_specs=in_specs,
        out_specs=out_specs,
        out_shape=out_shape,
        scratch_shapes=[pltpu.VMEM((SSM_WIDTH // LANES, b * tt, LANES), F32)],
        compiler_params=_cparams(("parallel",)),
        name=name,
    )(*args)


def _attn_scores(q_ref, kc_ref, k_ref, bias_ref, r0, q0, seq_len):
    pair_w = ATTN_GROUP * HEAD_DIM
    head_of_lane = lax.broadcasted_iota(jnp.int32, (1, pair_w), 1) // HEAD_DIM
    nt_dims = (((1,), (1,)), ((), ()))
    local = k_ref is not None
    if local:
        start = pl.multiple_of(jnp.clip(q0 - BLOCK, 0, seq_len - 3 * BLOCK), BLOCK)
        variant = jnp.where(q0 == 0, 0, jnp.where(q0 == seq_len - BLOCK, 2, 1))
        bias1 = bias_ref[variant]
        bias = jnp.concatenate([bias1] * ATTN_GROUP, axis=0)
    out = []
    for hk in range(KV_HEADS):
        lanes = slice(hk * pair_w, (hk + 1) * pair_w)
        kv_lanes = slice(hk * 2 * HEAD_DIM, (hk + 1) * 2 * HEAD_DIM)
        qb = q_ref[0, pl.ds(r0, BLOCK), lanes]
        zero = jnp.zeros_like(qb)
        qs = jnp.concatenate(
            [jnp.where(head_of_lane == g, qb, zero) for g in range(ATTN_GROUP)], axis=0)
        kc = kc_ref[0, :, kv_lanes]
        kc4 = jnp.concatenate([kc, kc], axis=1)
        s_ctx = lax.dot_general(qs, kc4, nt_dims, preferred_element_type=F32)
        if local:
            kl = k_ref[0, pl.ds(start, 3 * BLOCK), kv_lanes]
            kl4 = jnp.concatenate([kl, kl], axis=1)
            s_loc = lax.dot_general(qs, kl4, nt_dims, preferred_element_type=F32) + bias
            out.append((s_ctx, s_loc))
        else:
            out.append((s_ctx,))
    return out


def _attn_finish(scores, sink_ref, vc_ref, v_ref, ga_ref, o_ref, r0, q0, seq_len):
    pair_w = ATTN_GROUP * HEAD_DIM
    head_of_lane = lax.broadcasted_iota(jnp.int32, (1, pair_w), 1) // HEAD_DIM
    local = v_ref is not None
    if local:
        start = pl.multiple_of(jnp.clip(q0 - BLOCK, 0, seq_len - 3 * BLOCK), BLOCK)
    for hk in range(KV_HEADS):
        lanes = slice(hk * pair_w, (hk + 1) * pair_w)
        kv_lanes = slice(hk * 2 * HEAD_DIM, (hk + 1) * 2 * HEAD_DIM)
        parts = scores[hk]
        tiles = [p[:, t * LANES:(t + 1) * LANES] for p in parts for t in range(p.shape[1] // LANES)]
        sink_col = jnp.concatenate(
            [jnp.full((BLOCK, 1), sink_ref[hk * ATTN_GROUP + g] * LOG2E, F32)
             for g in range(ATTN_GROUP)], axis=0)
        mt = tiles[0]
        for tl in tiles[1:]:
            mt = jnp.maximum(mt, tl)
        m = jnp.maximum(jnp.max(mt, axis=-1, keepdims=True), sink_col)
        es = [jnp.exp2(p - m) for p in parts]
        et = None
        for e in es:
            for t in range(e.shape[1] // LANES):
                tl = e[:, t * LANES:(t + 1) * LANES]
                et = tl if et is None else et + tl
        den = jnp.sum(et, axis=-1, keepdims=True) + jnp.exp2(sink_col - m)
        vc = vc_ref[0, :, kv_lanes]
        vc4 = jnp.concatenate([vc, vc], axis=1)
        o = jnp.dot(es[0].astype(BF16), vc4, preferred_element_type=F32)
        if local:
            vl = v_ref[0, pl.ds(start, 3 * BLOCK), kv_lanes]
            vl4 = jnp.concatenate([vl, vl], axis=1)
            o = o + jnp.dot(es[1].astype(BF16), vl4, preferred_element_type=F32)
        o = o * (1.0 / den)
        oh = jnp.zeros((BLOCK, pair_w), F32)
        for g in range(ATTN_GROUP):
            oh = oh + jnp.where(head_of_lane == g, o[g * BLOCK:(g + 1) * BLOCK, :], 0.0)
        gate = ga_ref[0, pl.ds(r0, BLOCK), lanes].astype(F32)
        o_ref[0, pl.ds(r0, BLOCK), lanes] = (oh * _silu(gate)).astype(o_ref.dtype)


def _attn_kernel(sink_ref, q_ref, kc_ref, vc_ref, ga_ref, bias_ref, k_ref, v_ref, o_ref,
                 sc_scr, sl_scr, *, tq, seq_len):
    qi = pl.program_id(1)
    nblk = tq // BLOCK

    def scores_to(i, slot):
        r0 = pl.multiple_of(i * BLOCK, BLOCK)
        sc = _attn_scores(q_ref, kc_ref, k_ref, bias_ref, r0, qi * tq + i * BLOCK, seq_len)
        for hk in range(KV_HEADS):
            sc_scr[slot, hk] = sc[hk][0]
            sl_scr[slot, hk] = sc[hk][1]

    def finish_from(i, slot):
        r0 = pl.multiple_of(i * BLOCK, BLOCK)
        sc = [(sc_scr[slot, hk], sl_scr[slot, hk]) for hk in range(KV_HEADS)]
        _attn_finish(sc, sink_ref, vc_ref, v_ref, ga_ref, o_ref, r0, qi * tq + i * BLOCK, seq_len)

    scores_to(0, 0)

    def body(j, carry):
        i = 2 * j
        scores_to(i + 1, 1)
        finish_from(i, 0)
        scores_to(i + 2, 0)
        finish_from(i + 1, 1)
        return carry

    lax.fori_loop(0, nblk // 2 - 1, body, 0)
    scores_to(nblk - 1, 1)
    finish_from(nblk - 2, 0)
    finish_from(nblk - 1, 1)


def _attn_ctx_kernel(sink_ref, q_ref, kc_ref, vc_ref, ga_ref, o_ref, *, tq):
    for i in range(tq // BLOCK):
        sc = _attn_scores(q_ref, kc_ref, None, None, i * BLOCK, 0, 0)
        _attn_finish(sc, sink_ref, vc_ref, None, ga_ref, o_ref, i * BLOCK, 0, 0)


def _band_bias():
    i = np.arange(BLOCK)[:, None]
    j = np.arange(3 * BLOCK)[None, :]
    tabs = []
    for off in (0, -BLOCK, -2 * BLOCK):
        valid = np.abs(i - (j + off)) <= WINDOW
        tabs.append(np.where(valid, 0.0, NEG_INF))
    return jnp.asarray(np.stack(tabs), F32)


def _attn_call(sink, q, kc4, vc4, ga, k4=None, v4=None, tq=ATTN_TQ, name="attn"):
    b, t, _ = q.shape
    lc = kc4.shape[1]
    local = k4 is not None
    tq = tq if local else t
    in_specs = [
        pl.BlockSpec(memory_space=pltpu.SMEM),
        pl.BlockSpec((1, tq, ATTN_WIDTH), lambda bi, ti: (bi, ti, 0)),
        pl.BlockSpec((1, lc, 2 * KV_WIDTH), lambda bi, ti: (bi, 0, 0)),
        pl.BlockSpec((1, lc, 2 * KV_WIDTH), lambda bi, ti: (bi, 0, 0)),
        pl.BlockSpec((1, tq, ATTN_WIDTH), lambda bi, ti: (bi, ti, 0)),
    ]
    args = [sink, q, kc4, vc4, ga]
    scratch = []
    if local:
        in_specs += [pl.BlockSpec((3, BLOCK, 3 * BLOCK), lambda bi, ti: (0, 0, 0))]
        in_specs += [pl.BlockSpec((1, t, 2 * KV_WIDTH), lambda bi, ti: (bi, 0, 0))] * 2
        args += [_band_bias(), k4, v4]
        rows = ATTN_GROUP * BLOCK
        scratch = [pltpu.VMEM((2, KV_HEADS, rows, lc), F32),
                   pltpu.VMEM((2, KV_HEADS, rows, 3 * BLOCK), F32)]
        body = functools.partial(_attn_kernel, tq=tq, seq_len=t)
    else:
        body = functools.partial(_attn_ctx_kernel, tq=tq)
    return pl.pallas_call(
        body,
        grid=(b, t // tq),
        in_specs=in_specs,
        out_specs=pl.BlockSpec((1, tq, ATTN_WIDTH), lambda bi, ti: (bi, ti, 0)),
        out_shape=jax.ShapeDtypeStruct((b, t, ATTN_WIDTH), BF16),
        scratch_shapes=scratch,
        compiler_params=_cparams(("parallel", "arbitrary")),
        name=name,
    )(*args)


S5_TILE = 512
CROWS = 16


def _s5_kernel(zc_ref, zl_ref, m_ref, s_ref, o_ref, a_ref, d_ref, yc_ref, yl_ref,
               hc_buf, hl_buf, s_buf):
    half = 2 * SSM_STATE

    def scan_tile(z_ref, h_buf, row0, nrows, direction, carry):
        cols = slice(direction * 2 * half, (direction + 1) * 2 * half)
        s_buf[0:nrows, :] = jnp.dot(z_ref[pl.ds(row0, nrows), :], s_ref[0, :, cols],
                                    preferred_element_type=F32)
        a_re = a_ref[0, 2 * direction:2 * direction + 1, :]
        a_im = a_ref[0, 2 * direction + 1:2 * direction + 2, :]
        nchunks = nrows // CROWS

        def step(i, c):
            h_re, h_im = c
            j = i if direction == 0 else nchunks - 1 - i
            lr = pl.multiple_of(j * CROWS, CROWS)
            gr = pl.multiple_of(row0 + j * CROWS, CROWS)
            h_buf[pl.ds(gr, CROWS), cols] = jnp.concatenate([h_re, h_im], axis=1).astype(BF16)
            s_re = s_buf[pl.ds(lr, CROWS), 0:half]
            s_im = s_buf[pl.ds(lr, CROWS), half:2 * half]
            n_re = a_re * h_re - a_im * h_im + s_re
            n_im = a_re * h_im + a_im * h_re + s_im
            return n_re, n_im

        return lax.fori_loop(0, nchunks, step, carry)

    nc_rows = zc_ref.shape[0]
    nl_rows = zl_ref.shape[0]
    zero = jnp.zeros((CROWS, half), F32)
    carry = scan_tile(zc_ref, hc_buf, 0, nc_rows, 0, (zero, zero))
    for t in range(nl_rows // S5_TILE):
        carry = scan_tile(zl_ref, hl_buf, t * S5_TILE, S5_TILE, 0, carry)
    carry = scan_tile(zc_ref, hc_buf, 0, nc_rows, 1, (zero, zero))
    for t in reversed(range(nl_rows // S5_TILE)):
        carry = scan_tile(zl_ref, hl_buf, t * S5_TILE, S5_TILE, 1, carry)

    def emit(z_ref, h_buf, y_ref, row0, nrows):
        z = z_ref[pl.ds(row0, nrows), :]
        y = jnp.dot(z, m_ref[0], preferred_element_type=F32)
        y = y + jnp.dot(h_buf[pl.ds(row0, nrows), :], o_ref[0], preferred_element_type=F32)
        y = y + z.astype(F32) * d_ref[0]
        y_ref[pl.ds(row0, nrows), :] = y.astype(y_ref.dtype)

    emit(zc_ref, hc_buf, yc_ref, 0, nc_rows)
    for t in range(nl_rows // S5_TILE):
        emit(zl_ref, hl_buf, yl_ref, t * S5_TILE, S5_TILE)


def _s5_call(zc, zl, m_op, s_op, o_op, a_op, d_op):
    rc, rl = zc.shape[0], zl.shape[0]
    pw = 2 * GFLAT
    row_spec = lambda r: pl.BlockSpec((r, pw), lambda p: (0, p))
    op_spec = pl.BlockSpec((1, pw, pw), lambda p: (p, 0, 0))
    return pl.pallas_call(
        _s5_kernel,
        grid=(PAIRS,),
        in_specs=[
            row_spec(rc), row_spec(rl), op_spec, op_spec, op_spec,
            pl.BlockSpec((1, 4, 2 * SSM_STATE), lambda p: (p, 0, 0)),
            pl.BlockSpec((1, 1, pw), lambda p: (p, 0, 0)),
        ],
        out_specs=[row_spec(rc), row_spec(rl)],
        out_shape=[jax.ShapeDtypeStruct(zc.shape, F32), jax.ShapeDtypeStruct(zl.shape, F32)],
        scratch_shapes=[
            pltpu.VMEM((rc, pw), BF16),
            pltpu.VMEM((rl, pw), BF16),
            pltpu.VMEM((S5_TILE, 2 * 2 * SSM_STATE), F32),
        ],
        compiler_params=_cparams(("parallel",)),
        name="s5_scan",
    )(zc, zl, m_op, s_op, o_op, a_op, d_op)


def _s5_operators(a_re, a_im, log_dt, b_re, b_im, c_re, c_im, d_skip):
    t = CHUNK
    ne = a_re.shape[0]
    hp = lax.Precision.HIGHEST
    ar, ai = a_re.astype(F32), a_im.astype(F32)
    dt = jnp.exp(log_dt.astype(F32))[..., None]
    lr, li = (ar * dt)[..., None, :], (ai * dt)[..., None, :]

    def apow(expo):
        ee = jnp.asarray(np.asarray(expo, np.float32)).reshape(1, 2, 1, -1, 1)
        mag = jnp.exp(lr * ee)
        return mag * jnp.cos(li * ee), mag * jnp.sin(li * ee)

    one = np.ones((2, 1))
    a1r, a1i = apow(one)
    xr, xi = a1r[..., 0, :] - 1.0, a1i[..., 0, :]
    den = ar * ar + ai * ai
    cr, ci = (xr * ar + xi * ai) / den, (xi * ar - xr * ai) / den
    br = jnp.swapaxes(b_re.astype(F32), -1, -2)
    bi = jnp.swapaxes(b_im.astype(F32), -1, -2)
    bbr = cr[..., None, :] * br - ci[..., None, :] * bi
    bbi = cr[..., None, :] * bi + ci[..., None, :] * br
    ccr, cci = c_re.astype(F32), c_im.astype(F32)

    def rows_of(expo, vr, vi):
        pr, pi = apow(expo)
        pr, pi = pr[..., :, None, :], pi[..., :, None, :]
        vr, vi = vr[..., None, :, :], vi[..., None, :, :]
        shape = vr.shape[:3] + (GFLAT, SSM_STATE)
        return (pr * vr - pi * vi).reshape(shape), (pr * vi + pi * vr).reshape(shape)

    k = np.arange(t)
    mrr, mri = rows_of(np.stack([-k, k]), bbr, bbi)
    mcr, mci = rows_of(np.stack([k, -k]), ccr, cci)
    m_dir = (jnp.einsum('ndgrp,ndgqp->ndgrq', mrr, mcr, precision=hp)
             - jnp.einsum('ndgrp,ndgqp->ndgrq', mri, mci, precision=hp))
    s_idx = np.arange(GFLAT)[:, None] // SSM_GROUP
    t_idx = np.arange(GFLAT)[None, :] // SSM_GROUP
    m_g = (jnp.where(t_idx >= s_idx, m_dir[:, 0], 0.0)
           + jnp.where(s_idx >= t_idx, m_dir[:, 1], 0.0))
    srr, sri = rows_of(np.stack([t - 1 - k, k]), bbr, bbi)
    orr, ori = rows_of(np.stack([k + 1, t - k]), ccr, cci)
    atr, ati = apow(t * one)

    def pair_diag(x):
        x = x.reshape(ne, PAIRS, 2, x.shape[-2], x.shape[-1])
        z = jnp.zeros_like(x[:, :, 0])
        top = jnp.concatenate([x[:, :, 0], z], axis=-1)
        bot = jnp.concatenate([z, x[:, :, 1]], axis=-1)
        return jnp.concatenate([top, bot], axis=-2)

    m_pair = pair_diag(m_g)
    s_pair = jnp.concatenate([pair_diag(v[:, d]) for d in range(2) for v in (srr, sri)], axis=-1)
    o_pair = jnp.concatenate([pair_diag(jnp.swapaxes(v[:, d], -1, -2))
                              for d in range(2) for v in (orr, -ori)], axis=-2)
    a_pair = jnp.stack([v[:, d, :, 0, :].reshape(ne, PAIRS, 2 * SSM_STATE)
                        for d in range(2) for v in (atr, ati)], axis=2)
    d_g = d_skip.astype(F32).reshape(ne, PAIRS, 2, 1, SSM_GROUP)
    d_flat = jnp.broadcast_to(d_g, (ne, PAIRS, 2, CHUNK, SSM_GROUP)).reshape(ne, PAIRS, 1, 2 * GFLAT)
    return m_pair.astype(BF16), s_pair.astype(BF16), o_pair.astype(BF16), a_pair, d_flat


def _even_out_kernel(og_ref, yf_ref, gs_ref, h_ref, mod_ref, gw_ref, gb_ref, wo_ref, o_ref, y_scr):
    nb, tt, d = h_ref.shape
    rows = nb * tt
    for kk in range(tt // CHUNK):
        for j in range(SSM_WIDTH // LANES):
            for hh in range(CHUNK // PIECES):
                v = []
                for qi in range(PIECES):
                    col = (j * PIECES + qi) * GFLAT + hh * LANES
                    v.append(yf_ref[kk * nb:(kk + 1) * nb, col:col + LANES])
                w = _piece_transpose(v)
                for i in range(PIECES):
                    tok = kk * CHUNK + hh * PIECES + i
                    y_scr[j, pl.ds(tok, nb, stride=tt), :] = w[i]
    y = jnp.concatenate([y_scr[j] for j in range(SSM_WIDTH // LANES)], axis=1)
    z = _gelu_tanh(y)
    t = jnp.dot(z.astype(BF16), gw_ref[...], preferred_element_type=F32) + gb_ref[...]
    gs = gs_ref[...].reshape(rows, SSM_WIDTH).astype(F32)
    os_ = z * _sigmoid(t) * _silu(gs)
    og = og_ref[...].reshape(rows, ATTN_WIDTH)
    yo = jnp.dot(og, wo_ref[0:ATTN_WIDTH, :], preferred_element_type=F32)
    yo = yo + jnp.dot(os_.astype(BF16), wo_ref[ATTN_WIDTH:, :], preferred_element_type=F32)
    o_ref[...] = h_ref[...] + mod_ref[:, 2:3, :] * yo.reshape(nb, tt, d)


def _even_out_call(og, yflat, gs, h, mod, gw, gb, wo, name="even_out"):
    b, t, d = h.shape
    tt = TOK_BLK
    bm = mod.shape[0]
    tok = lambda w: pl.BlockSpec((b, tt, w), lambda ti: (0, ti, 0))
    full = lambda r, c: pl.BlockSpec((r, c), lambda ti: (0, 0))
    return pl.pallas_call(
        _even_out_kernel,
        grid=(t // tt,),
        in_specs=[tok(ATTN_WIDTH),
                  pl.BlockSpec(((tt // CHUNK) * b, FLAT_W), lambda ti: (ti, 0)),
                  tok(SSM_WIDTH), tok(d),
                  pl.BlockSpec((bm, 3, d), lambda ti: (0, 0, 0)),
                  full(SSM_WIDTH, SSM_WIDTH), full(1, SSM_WIDTH), full(d, d)],
        out_specs=tok(d),
        out_shape=jax.ShapeDtypeStruct((b, t, d), F32),
        scratch_shapes=[pltpu.VMEM((SSM_WIDTH // LANES, b * tt, LANES), F32)],
        compiler_params=_cparams(("parallel",)),
        name=name,
    )(og, yflat, gs, h, mod, gw, gb.reshape(1, SSM_WIDTH), wo)


def _window_sum(x, r):
    n = x.shape[0]
    up = lambda v, k: pltpu.roll(v, n - k, 0)
    acc, span = x, 1
    while span * 2 <= r * 2:
        acc = acc + up(acc, span)
        span *= 2
    return pltpu.roll(acc, r, 0) + up(x, r)


def _odd_kernel(x_ref, xp_ref, xn_ref, mod_ref, g_ref, wi_ref, pw_ref, ps_ref, wo_ref, fg_ref,
                o_ref, *, tm, seq_len, final):
    ti = pl.program_id(1)
    nt = pl.num_programs(1)
    d = x_ref.shape[-1]
    x = x_ref[0]
    x_ext = jnp.concatenate([xp_ref[0], x, xn_ref[0]], axis=0)
    r = lax.rsqrt(jnp.mean(x_ext * x_ext, axis=-1, keepdims=True) + EPS)
    a_ext = (x_ext * r * g_ref[...]) * (1.0 + mod_ref[0, 1:2, :]) + mod_ref[0, 0:1, :]
    ab = a_ext.astype(BF16)
    u_ext = jnp.dot(ab, wi_ref[:, 0:d], preferred_element_type=F32)
    row = lax.broadcasted_iota(jnp.int32, (tm + 2 * POOL_HALO, 1), 0)
    inside = jnp.logical_and(jnp.logical_or(ti > 0, row >= POOL_HALO),
                             jnp.logical_or(ti < nt - 1, row < POOL_HALO + tm))
    u_ext = jnp.where(inside, u_ext, 0.0)
    gate = jnp.dot(ab[POOL_HALO:POOL_HALO + tm], wi_ref[:, d:2 * d], preferred_element_type=F32)
    u = u_ext[POOL_HALO:POOL_HALO + tm]
    pos = ti * tm + lax.broadcasted_iota(jnp.int32, (tm, 1), 0)
    parts = []
    for gi, w in enumerate(POOL_WINDOWS):
        rad = w // 2
        lanes = slice(gi * POOL_GROUP, (gi + 1) * POOL_GROUP)
        ws = _window_sum(u_ext[:, lanes], rad)[POOL_HALO:POOL_HALO + tm]
        cnt = (jnp.minimum(pos + rad + 1, seq_len) - jnp.maximum(pos - rad, 0)).astype(F32)
        p = ws * (1.0 / cnt) - u[:, lanes]
        parts.append(jnp.dot(p.astype(BF16), pw_ref[gi], preferred_element_type=F32))
    p_all = jnp.concatenate(parts, axis=1) * ps_ref[...]
    mixed = p_all * _silu(gate)
    yo = jnp.dot(mixed.astype(BF16), wo_ref[...], preferred_element_type=F32)
    hn = x + mod_ref[0, 2:3, :] * yo
    if final:
        rr = lax.rsqrt(jnp.mean(hn * hn, axis=-1, keepdims=True) + EPS)
        hn = hn * rr * fg_ref[...]
    o_ref[0] = hn


def _odd_call(h, mod, g, w_in, pool_w, pool_scale, wo, final_g, tm, final, name="odd"):
    b, t, d = h.shape
    hb = tm // POOL_HALO
    nhalo = t // POOL_HALO
    mod_map = (lambda bi, ti: (bi, 0, 0)) if mod.shape[0] == b else (lambda bi, ti: (0, 0, 0))
    tok = pl.BlockSpec((1, tm, d), lambda bi, ti: (bi, ti, 0))
    vec = pl.BlockSpec((1, d), lambda bi, ti: (0, 0))
    return pl.pallas_call(
        functools.partial(_odd_kernel, tm=tm, seq_len=t, final=final),
        grid=(b, t // tm),
        in_specs=[
            tok,
            pl.BlockSpec((1, POOL_HALO, d), lambda bi, ti: (bi, jnp.maximum(ti * hb - 1, 0), 0)),
            pl.BlockSpec((1, POOL_HALO, d), lambda bi, ti: (bi, jnp.minimum((ti + 1) * hb, nhalo - 1), 0)),
            pl.BlockSpec((1, 3, d), mod_map),
            vec,
            pl.BlockSpec((d, 2 * d), lambda bi, ti: (0, 0)),
            pl.BlockSpec((len(POOL_WINDOWS), POOL_GROUP, POOL_GROUP), lambda bi, ti: (0, 0, 0)),
            vec,
            pl.BlockSpec((d, d), lambda bi, ti: (0, 0)),
            vec,
        ],
        out_specs=tok,
        out_shape=jax.ShapeDtypeStruct((b, t, d), F32),
        compiler_params=_cparams(("parallel", "arbitrary")),
        name=name,
    )(h, h, h, mod, g.reshape(1, d), w_in, pool_w, pool_scale.reshape(1, d), wo, final_g.reshape(1, d))


def _rope_tables(n_tokens):
    pos = np.arange(n_tokens)
    row = (pos // GRID_W).astype(np.float32)
    col = (pos % GRID_W).astype(np.float32)
    inv_freq = ROPE_BASE ** (-jnp.arange(ROPE_FREQS, dtype=F32) / ROPE_FREQS)
    ang_r = jnp.asarray(row)[:, None] * inv_freq
    ang_c = jnp.asarray(col)[:, None] * inv_freq
    cos_h = jnp.concatenate([jnp.cos(ang_r)] * 2 + [jnp.cos(ang_c)] * 2, axis=1)
    sin_h = jnp.concatenate([-jnp.sin(ang_r), jnp.sin(ang_r), -jnp.sin(ang_c), jnp.sin(ang_c)], axis=1)
    return jnp.tile(cos_h, (1, 2)), jnp.tile(sin_h, (1, 2))


def _even_weights(w_in):
    c = np.cumsum([0, ATTN_WIDTH, KV_WIDTH, KV_WIDTH, ATTN_WIDTH, SSM_WIDTH, SSM_WIDTH])
    wq = w_in[:, c[0]:c[1]]
    wk = w_in[:, c[1]:c[2]]
    wv = w_in[:, c[2]:c[3]]
    dup = lambda w: jnp.concatenate([w[:, :HEAD_DIM]] * 2 + [w[:, HEAD_DIM:]] * 2, axis=1)
    w = jnp.concatenate([wq, dup(wk), dup(wv), w_in[:, c[3]:]], axis=1)
    return w.astype(BF16)


def kernel(x, c, ctx, c_ctx, ada_w, ada_b, norm_g, even_w_in, even_w_out, attn_sink,
           ssm_a_re, ssm_a_im, ssm_log_dt, ssm_b_re, ssm_b_im, ssm_c_re, ssm_c_im, ssm_d,
           glu_w, glu_b, odd_w_in, odd_w_out, pool_w, pool_scale, final_g):
    b, l, d = x.shape
    lc = ctx.shape[1]
    tm = 512
    rope_tabs = _rope_tables(l)

    s_all = jnp.zeros((24, d), F32).at[:b].set(c).at[b].set(c_ctx)
    mods = _ada_call(s_all, ada_w, ada_b).reshape(DEPTH, 24, 3, d)
    s5_ops = _s5_operators(ssm_a_re, ssm_a_im, ssm_log_dt, ssm_b_re, ssm_b_im,
                           ssm_c_re, ssm_c_im, ssm_d)
    h, hc = x, ctx
    for i in range(DEPTH):
        need_ctx = i < DEPTH - 1
        mod = mods[i, :b]
        mod_c = mods[i, b:b + 1]
        j = i // 2
        if i % 2 == 0:
            w_in = _even_weights(even_w_in[j])
            wo = even_w_out[j].astype(BF16)
            gw = glu_w[j].astype(BF16)
            sink = attn_sink[j].astype(F32)
            q, k4, v4, ga, zl, gs = _even_in_call(h, mod, norm_g[i], w_in, rope_tabs, name="even_in")
            qc, kc4, vc4, gac, zc, gsc = _even_in_call(hc, mod_c, norm_g[i], w_in, None,
                                                       name="even_in_ctx")
            og = _attn_call(sink, q, kc4, vc4, ga, k4, v4, name="attn")
            yc_f, yl_f = _s5_call(zc, zl, *[op[j] for op in s5_ops])
            h_new = _even_out_call(og, yl_f, gs, h, mod, gw, glu_b[j], wo, name="even_out")
            if need_ctx:
                ogc = _attn_call(sink, qc, kc4, vc4, gac, name="attn_ctx")
                hc = _even_out_call(ogc, yc_f, gsc, hc, mod_c, gw, glu_b[j], wo, name="even_out_ctx")
            h = h_new
        else:
            w_in = odd_w_in[j].astype(BF16)
            wo = odd_w_out[j].astype(BF16)
            pw = pool_w[j].astype(BF16)
            final = i == DEPTH - 1
            h_new = _odd_call(h, mod, norm_g[i], w_in, pw, pool_scale[j], wo, final_g, tm, final,
                              name="odd")
            if need_ctx:
                hc = _odd_call(hc, mod_c, norm_g[i], w_in, pw, pool_scale[j], wo, final_g, lc, False,
                               name="odd_ctx")
            h = h_new
    return h
```

```python
import functools
import math

import numpy as np
import jax
import jax.numpy as jnp
from jax import lax
from jax.experimental import pallas as pl
from jax.experimental.pallas import tpu as pltpu

D_MODEL = 1024
DEPTH = 4
GRID_W = 64
EPS = 1e-6
NEG_INF = -1e30
LOG2E = math.log2(math.e)

HEAD_DIM = 64
ATTN_HEADS = 8
KV_HEADS = 2
ATTN_GROUP = ATTN_HEADS // KV_HEADS
ATTN_WIDTH = ATTN_HEADS * HEAD_DIM
KV_WIDTH = KV_HEADS * HEAD_DIM
WINDOW = 128
BLOCK = 128
ROPE_BASE = 10000.0
ROPE_FREQS = HEAD_DIM // 4

SSM_WIDTH = 512
SSM_GROUP = 16
SSM_GROUPS = 32
SSM_STATE = 64
CHUNK = 16
GFLAT = CHUNK * SSM_GROUP
PAIRS = SSM_GROUPS // 2
FLAT_W = SSM_GROUPS * GFLAT

POOL_WINDOWS = (2, 4, 8, 16)
POOL_GROUP = D_MODEL // len(POOL_WINDOWS)
POOL_HALO = 8

LANES = 128
PIECES = LANES // SSM_GROUP
VMEM_LIMIT = 56 * 1024 * 1024
ATTN_TQ = 4096
TOK_BLK = 32

F32 = jnp.float32
BF16 = jnp.bfloat16
YFLAT_DTYPE = BF16


def _sigmoid(x):
    return 0.5 * jnp.tanh(0.5 * x) + 0.5


def _silu(x):
    return x * _sigmoid(x)


def _gelu_tanh(x):
    c = math.sqrt(2.0 / math.pi)
    return 0.5 * x * (1.0 + jnp.tanh(c * (x + 0.044715 * (x * x * x))))


def _cparams(sem):
    return pltpu.CompilerParams(dimension_semantics=sem, vmem_limit_bytes=VMEM_LIMIT)


def _ada_kernel(s_ref, w_ref, b_ref, o_ref):
    s = _silu(s_ref[...]).astype(BF16)
    y = jnp.dot(s, w_ref[0].astype(BF16), preferred_element_type=F32)
    o_ref[0] = y + b_ref[0]


def _ada_call(s_all, ada_w, ada_b):
    rows = s_all.shape[0]
    n = ada_w.shape[-1]
    tn = D_MODEL
    return pl.pallas_call(
        _ada_kernel,
        grid=(DEPTH, n // tn),
        in_specs=[
            pl.BlockSpec((rows, D_MODEL), lambda i, j: (0, 0)),
            pl.BlockSpec((1, D_MODEL, tn), lambda i, j: (i, 0, j)),
            pl.BlockSpec((1, 1, tn), lambda i, j: (i, 0, j)),
        ],
        out_specs=pl.BlockSpec((1, rows, tn), lambda i, j: (i, 0, j)),
        out_shape=jax.ShapeDtypeStruct((DEPTH, rows, n), F32),
        compiler_params=_cparams(("arbitrary", "arbitrary")),
        name="ada_mod",
    )(s_all, ada_w, ada_b.reshape(DEPTH, 1, n))


def _rope(y, cos, sin_signed):
    lane = lax.broadcasted_iota(jnp.int32, (1, LANES), 1)
    first_half = (lane % (2 * ROPE_FREQS)) < ROPE_FREQS
    outs = []
    for j in range(y.shape[1] // LANES):
        yj = y[:, j * LANES:(j + 1) * LANES]
        up = pltpu.roll(yj, LANES - ROPE_FREQS, 1)
        dn = pltpu.roll(yj, ROPE_FREQS, 1)
        partner = jnp.where(first_half, up, dn)
        outs.append(yj * cos + partner * sin_signed)
    return jnp.concatenate(outs, axis=1)


def _piece_transpose(v):
    piece = lax.broadcasted_iota(jnp.int32, (1, LANES), 1) // SSM_GROUP
    v = list(v)
    for dist in (4, 2, 1):
        keep = (piece & dist) == 0
        nv = list(v)
        for i in range(PIECES):
            if i & dist == 0:
                a, b = v[i], v[i + dist]
                nv[i] = jnp.where(keep, a, pltpu.roll(b, dist * SSM_GROUP, 1))
                nv[i + dist] = jnp.where(keep, pltpu.roll(a, LANES - dist * SSM_GROUP, 1), b)
        v = nv
    return v


def _norm_mod(x_ref, mod_ref, g_ref):
    x = x_ref[...]
    r = lax.rsqrt(jnp.mean(x * x, axis=-1, keepdims=True) + EPS)
    shift = mod_ref[:, 0:1, :]
    scale = mod_ref[:, 1:2, :]
    return (x * r * g_ref[...]) * (1.0 + scale) + shift


EV_Q, EV_K, EV_V, EV_GA, EV_U, EV_GS = 0, 512, 768, 1024, 1536, 2048
EV_N = 2560


def _even_in_kernel(*refs, use_rope):
    if use_rope:
        x_ref, mod_ref, g_ref, w_ref, cos_ref, sin_ref = refs[:6]
        refs = refs[6:]
    else:
        x_ref, mod_ref, g_ref, w_ref = refs[:4]
        refs = refs[4:]
    q_ref, k_ref, v_ref, ga_ref, z_ref, gs_ref, u_scr = refs
    nb, tt, d = x_ref.shape
    rows = nb * tt
    ab = _norm_mod(x_ref, mod_ref, g_ref).reshape(rows, d).astype(BF16)

    def proj(start, width):
        return jnp.dot(ab, w_ref[:, start:start + width], preferred_element_type=F32)

    def put(o_ref, y):
        o_ref[...] = y.astype(o_ref.dtype).reshape(o_ref.shape)

    u = proj(EV_U, 512)
    for j in range(SSM_WIDTH // LANES):
        u_scr[j] = u[:, j * LANES:(j + 1) * LANES]
    q = proj(EV_Q, 512)
    k = proj(EV_K, 256)
    for kk in range(tt // CHUNK):
        for j in range(SSM_WIDTH // LANES):
            for hh in range(CHUNK // PIECES):
                w = _piece_transpose(
                    [pltpu.bitcast(
                        u_scr[j, pl.ds(kk * CHUNK + hh * PIECES + i, nb, stride=tt), :].astype(BF16),
                        jnp.uint32) for i in range(PIECES)])
                for qi in range(PIECES):
                    g = j * PIECES + qi
                    col = g * GFLAT + hh * LANES
                    z_ref[kk * nb:(kk + 1) * nb, col:col + LANES] = pltpu.bitcast(w[qi], BF16)
    if use_rope:
        cos = jnp.concatenate([cos_ref[...]] * nb, axis=0)
        sin = jnp.concatenate([sin_ref[...]] * nb, axis=0)
        q = _rope(q, cos, sin)
        k = _rope(k, cos, sin)
    put(q_ref, q * (HEAD_DIM ** -0.5 * LOG2E))
    put(k_ref, k)
    put(v_ref, proj(EV_V, 256))
    put(ga_ref, proj(EV_GA, 512))
    put(gs_ref, proj(EV_GS, 512))


def _even_in_call(x, mod, g, w, rope_tabs=None, name="even_in"):
    b, t, d = x.shape
    tt = TOK_BLK
    use_rope = rope_tabs is not None
    bm = mod.shape[0]
    tok = lambda wd: pl.BlockSpec((b, tt, wd), lambda ti: (0, ti, 0))
    in_specs = [
        tok(d),
        pl.BlockSpec((bm, 3, d), lambda ti: (0, 0, 0)),
        pl.BlockSpec((1, d), lambda ti: (0, 0)),
        pl.BlockSpec((d, EV_N), lambda ti: (0, 0)),
    ]
    args = [x, mod, g.reshape(1, d), w]
    if use_rope:
        in_specs += [pl.BlockSpec((tt, LANES), lambda ti: (ti, 0))] * 2
        args += list(rope_tabs)
    flat_rows = (tt // CHUNK) * b
    widths = (512, 256, 256, 512, None, 512)
    out_specs, out_shape = [], []
    for wd in widths:
        if wd is None:
            out_specs.append(pl.BlockSpec((flat_rows, FLAT_W), lambda ti: (ti, 0)))
            out_shape.append(jax.ShapeDtypeStruct(((t // CHUNK) * b, FLAT_W), BF16))
        else:
            out_specs.append(tok(wd))
            out_shape.append(jax.ShapeDtypeStruct((b, t, wd), BF16))
    return pl.pallas_call(
        functools.partial(_even_in_kernel, use_rope=use_rope),
        grid=(t // tt,),
        in_specs=in_specs,
        out_specs=out_specs,
        out_shape=out_shape,
        scratch_shapes=[pltpu.VMEM((SSM_WIDTH // LANES, b * tt, LANES), F32)],
        compiler_params=_cparams(("parallel",)),
        name=name,
    )(*args)


def _attn_scores(q_ref, kc_ref, k_ref, bias_ref, r0, q0, seq_len):
    pair_w = ATTN_GROUP * HEAD_DIM
    head_of_lane = lax.broadcasted_iota(jnp.int32, (1, pair_w), 1) // HEAD_DIM
    nt_dims = (((1,), (1,)), ((), ()))
    local = k_ref is not None
    if local:
        start = pl.multiple_of(jnp.clip(q0 - BLOCK, 0, seq_len - 3 * BLOCK), BLOCK)
        variant = jnp.where(q0 == 0, 0, jnp.where(q0 == seq_len - BLOCK, 2, 1))
        bias1 = bias_ref[variant]
        bias = jnp.concatenate([bias1] * ATTN_GROUP, axis=0)
    out = []
    for hk in range(KV_HEADS):
        lanes = slice(hk * pair_w, (hk + 1) * pair_w)
        kv_lanes = slice(hk * 2 * HEAD_DIM, (hk + 1) * 2 * HEAD_DIM)
        qb = q_ref[0, pl.ds(r0, BLOCK), lanes]
        zero = jnp.zeros_like(qb)
        qs = jnp.concatenate(
            [jnp.where(head_of_lane == g, qb, zero) for g in range(ATTN_GROUP)], axis=0)
        kc = kc_ref[0, :, kv_lanes]
        kc4 = jnp.concatenate([kc, kc], axis=1)
        s_ctx = lax.dot_general(qs, kc4, nt_dims, preferred_element_type=F32)
        if local:
            kl = k_ref[0, pl.ds(start, 3 * BLOCK), kv_lanes]
            kl4 = jnp.concatenate([kl, kl], axis=1)
            s_loc = lax.dot_general(qs, kl4, nt_dims, preferred_element_type=F32) + bias
            out.append((s_ctx, s_loc))
        else:
            out.append((s_ctx,))
    return out


def _attn_finish(scores, sink_ref, vc_ref, v_ref, ga_ref, o_ref, r0, q0, seq_len):
    pair_w = ATTN_GROUP * HEAD_DIM
    head_of_lane = lax.broadcasted_iota(jnp.int32, (1, pair_w), 1) // HEAD_DIM
    local = v_ref is not None
    if local:
        start = pl.multiple_of(jnp.clip(q0 - BLOCK, 0, seq_len - 3 * BLOCK), BLOCK)
    for hk in range(KV_HEADS):
        lanes = slice(hk * pair_w, (hk + 1) * pair_w)
        kv_lanes = slice(hk * 2 * HEAD_DIM, (hk + 1) * 2 * HEAD_DIM)
        parts = scores[hk]
        tiles = [p[:, t * LANES:(t + 1) * LANES] for p in parts for t in range(p.shape[1] // LANES)]
        sink_col = jnp.concatenate(
            [jnp.full((BLOCK, 1), sink_ref[hk * ATTN_GROUP + g] * LOG2E, F32)
             for g in range(ATTN_GROUP)], axis=0)
        mt = tiles[0]
        for tl in tiles[1:]:
            mt = jnp.maximum(mt, tl)
        m = jnp.maximum(jnp.max(mt, axis=-1, keepdims=True), sink_col)
        es = [jnp.exp2(p - m) for p in parts]
        et = None
        for e in es:
            for t in range(e.shape[1] // LANES):
                tl = e[:, t * LANES:(t + 1) * LANES]
                et = tl if et is None else et + tl
        den = jnp.sum(et, axis=-1, keepdims=True) + jnp.exp2(sink_col - m)
        vc = vc_ref[0, :, kv_lanes]
        vc4 = jnp.concatenate([vc, vc], axis=1)
        o = jnp.dot(es[0].astype(BF16), vc4, preferred_element_type=F32)
        if local:
            vl = v_ref[0, pl.ds(start, 3 * BLOCK), kv_lanes]
            vl4 = jnp.concatenate([vl, vl], axis=1)
            o = o + jnp.dot(es[1].astype(BF16), vl4, preferred_element_type=F32)
        o = o * (1.0 / den)
        oh = jnp.zeros((BLOCK, pair_w), F32)
        for g in range(ATTN_GROUP):
            oh = oh + jnp.where(head_of_lane == g, o[g * BLOCK:(g + 1) * BLOCK, :], 0.0)
        gate = ga_ref[0, pl.ds(r0, BLOCK), lanes].astype(F32)
        o_ref[0, pl.ds(r0, BLOCK), lanes] = (oh * _silu(gate)).astype(o_ref.dtype)


def _attn_kernel(sink_ref, q_ref, kc_ref, vc_ref, ga_ref, bias_ref, k_ref, v_ref, o_ref,
                 sc_scr, sl_scr, *, tq, seq_len):
    qi = pl.program_id(1)
    nblk = tq // BLOCK

    def scores_to(i, slot):
        r0 = pl.multiple_of(i * BLOCK, BLOCK)
        sc = _attn_scores(q_ref, kc_ref, k_ref, bias_ref, r0, qi * tq + i * BLOCK, seq_len)
        for hk in range(KV_HEADS):
            sc_scr[slot, hk] = sc[hk][0]
            sl_scr[slot, hk] = sc[hk][1]

    def finish_from(i, slot):
        r0 = pl.multiple_of(i * BLOCK, BLOCK)
        sc = [(sc_scr[slot, hk], sl_scr[slot, hk]) for hk in range(KV_HEADS)]
        _attn_finish(sc, sink_ref, vc_ref, v_ref, ga_ref, o_ref, r0, qi * tq + i * BLOCK, seq_len)

    scores_to(0, 0)

    def body(j, carry):
        i = 2 * j
        scores_to(i + 1, 1)
        finish_from(i, 0)
        scores_to(i + 2, 0)
        finish_from(i + 1, 1)
        return carry

    lax.fori_loop(0, nblk // 2 - 1, body, 0)
    scores_to(nblk - 1, 1)
    finish_from(nblk - 2, 0)
    finish_from(nblk - 1, 1)


def _attn_ctx_kernel(sink_ref, q_ref, kc_ref, vc_ref, ga_ref, o_ref, *, tq):
    for i in range(tq // BLOCK):
        sc = _attn_scores(q_ref, kc_ref, None, None, i * BLOCK, 0, 0)
        _attn_finish(sc, sink_ref, vc_ref, None, ga_ref, o_ref, i * BLOCK, 0, 0)


def _band_bias():
    i = np.arange(BLOCK)[:, None]
    j = np.arange(3 * BLOCK)[None, :]
    tabs = []
    for off in (0, -BLOCK, -2 * BLOCK):
        valid = np.abs(i - (j + off)) <= WINDOW
        tabs.append(np.where(valid, 0.0, NEG_INF))
    return jnp.asarray(np.stack(tabs), F32)


def _attn_call(sink, q, kc4, vc4, ga, k4=None, v4=None, tq=ATTN_TQ, name="attn"):
    b, t, _ = q.shape
    lc = kc4.shape[1]
    local = k4 is not None
    tq = tq if local else t
    in_specs = [
        pl.BlockSpec(memory_space=pltpu.SMEM),
        pl.BlockSpec((1, tq, ATTN_WIDTH), lambda bi, ti: (bi, ti, 0)),
        pl.BlockSpec((1, lc, 2 * KV_WIDTH), lambda bi, ti: (bi, 0, 0)),
        pl.BlockSpec((1, lc, 2 * KV_WIDTH), lambda bi, ti: (bi, 0, 0)),
        pl.BlockSpec((1, tq, ATTN_WIDTH), lambda bi, ti: (bi, ti, 0)),
    ]
    args = [sink, q, kc4, vc4, ga]
    scratch = []
    if local:
        in_specs += [pl.BlockSpec((3, BLOCK, 3 * BLOCK), lambda bi, ti: (0, 0, 0))]
        in_specs += [pl.BlockSpec((1, t, 2 * KV_WIDTH), lambda bi, ti: (bi, 0, 0))] * 2
        args += [_band_bias(), k4, v4]
        rows = ATTN_GROUP * BLOCK
        scratch = [pltpu.VMEM((2, KV_HEADS, rows, lc), F32),
                   pltpu.VMEM((2, KV_HEADS, rows, 3 * BLOCK), F32)]
        body = functools.partial(_attn_kernel, tq=tq, seq_len=t)
    else:
        body = functools.partial(_attn_ctx_kernel, tq=tq)
    return pl.pallas_call(
        body,
        grid=(b, t // tq),
        in_specs=in_specs,
        out_specs=pl.BlockSpec((1, tq, ATTN_WIDTH), lambda bi, ti: (bi, ti, 0)),
        out_shape=jax.ShapeDtypeStruct((b, t, ATTN_WIDTH), BF16),
        scratch_shapes=scratch,
        compiler_params=_cparams(("parallel", "arbitrary")),
        name=name,
    )(*args)


S5_TILE = 512
CROWS = 16


def _s5_kernel(zc_ref, zl_ref, m_ref, s_ref, o_ref, a_ref, d_ref, yc_ref, yl_ref,
               hc_buf, hl_buf, s_buf):
    half = 2 * SSM_STATE

    def scan_tile(z_ref, h_buf, row0, nrows, direction, carry):
        cols = slice(direction * 2 * half, (direction + 1) * 2 * half)
        s_buf[0:nrows, :] = jnp.dot(z_ref[pl.ds(row0, nrows), :], s_ref[0, :, cols],
                                    preferred_element_type=F32)
        a_re = a_ref[0, 2 * direction:2 * direction + 1, :]
        a_im = a_ref[0, 2 * direction + 1:2 * direction + 2, :]
        nchunks = nrows // CROWS

        def step(i, c):
            h_re, h_im = c
            j = i if direction == 0 else nchunks - 1 - i
            lr = pl.multiple_of(j * CROWS, CROWS)
            gr = pl.multiple_of(row0 + j * CROWS, CROWS)
            h_buf[pl.ds(gr, CROWS), cols] = jnp.concatenate([h_re, h_im], axis=1).astype(BF16)
            s_re = s_buf[pl.ds(lr, CROWS), 0:half]
            s_im = s_buf[pl.ds(lr, CROWS), half:2 * half]
            n_re = a_re * h_re - a_im * h_im + s_re
            n_im = a_re * h_im + a_im * h_re + s_im
            return n_re, n_im

        return lax.fori_loop(0, nchunks, step, carry)

    nc_rows = zc_ref.shape[0]
    nl_rows = zl_ref.shape[0]
    zero = jnp.zeros((CROWS, half), F32)
    carry = scan_tile(zc_ref, hc_buf, 0, nc_rows, 0, (zero, zero))
    for t in range(nl_rows // S5_TILE):
        carry = scan_tile(zl_ref, hl_buf, t * S5_TILE, S5_TILE, 0, carry)
    carry = scan_tile(zc_ref, hc_buf, 0, nc_rows, 1, (zero, zero))
    for t in reversed(range(nl_rows // S5_TILE)):
        carry = scan_tile(zl_ref, hl_buf, t * S5_TILE, S5_TILE, 1, carry)

    def emit(z_ref, h_buf, y_ref, row0, nrows):
        z = z_ref[pl.ds(row0, nrows), :]
        y = jnp.dot(z, m_ref[0], preferred_element_type=F32)
        y = y + jnp.dot(h_buf[pl.ds(row0, nrows), :], o_ref[0], preferred_element_type=F32)
        y = y + z.astype(F32) * d_ref[0]
        y_ref[pl.ds(row0, nrows), :] = y.astype(y_ref.dtype)

    emit(zc_ref, hc_buf, yc_ref, 0, nc_rows)
    for t in range(nl_rows // S5_TILE):
        emit(zl_ref, hl_buf, yl_ref, t * S5_TILE, S5_TILE)


def _s5_call(zc, zl, m_op, s_op, o_op, a_op, d_op):
    rc, rl = zc.shape[0], zl.shape[0]
    pw = 2 * GFLAT
    row_spec = lambda r: pl.BlockSpec((r, pw), lambda p: (0, p))
    op_spec = pl.BlockSpec((1, pw, pw), lambda p: (p, 0, 0))
    return pl.pallas_call(
        _s5_kernel,
        grid=(PAIRS,),
        in_specs=[
            row_spec(rc), row_spec(rl), op_spec, op_spec, op_spec,
            pl.BlockSpec((1, 4, 2 * SSM_STATE), lambda p: (p, 0, 0)),
            pl.BlockSpec((1, 1, pw), lambda p: (p, 0, 0)),
        ],
        out_specs=[row_spec(rc), row_spec(rl)],
        out_shape=[jax.ShapeDtypeStruct(zc.shape, YFLAT_DTYPE), jax.ShapeDtypeStruct(zl.shape, YFLAT_DTYPE)],
        scratch_shapes=[
            pltpu.VMEM((rc, pw), BF16),
            pltpu.VMEM((rl, pw), BF16),
            pltpu.VMEM((S5_TILE, 2 * 2 * SSM_STATE), F32),
        ],
        compiler_params=_cparams(("parallel",)),
        name="s5_scan",
    )(zc, zl, m_op, s_op, o_op, a_op, d_op)


def _s5_operators(a_re, a_im, log_dt, b_re, b_im, c_re, c_im, d_skip):
    t = CHUNK
    ne = a_re.shape[0]
    hp = lax.Precision.HIGHEST
    ar, ai = a_re.astype(F32), a_im.astype(F32)
    dt = jnp.exp(log_dt.astype(F32))[..., None]
    lr, li = (ar * dt)[..., None, :], (ai * dt)[..., None, :]

    def apow(expo):
        ee = jnp.asarray(np.asarray(expo, np.float32)).reshape(1, 2, 1, -1, 1)
        mag = jnp.exp(lr * ee)
        return mag * jnp.cos(li * ee), mag * jnp.sin(li * ee)

    one = np.ones((2, 1))
    a1r, a1i = apow(one)
    xr, xi = a1r[..., 0, :] - 1.0, a1i[..., 0, :]
    den = ar * ar + ai * ai
    cr, ci = (xr * ar + xi * ai) / den, (xi * ar - xr * ai) / den
    br = jnp.swapaxes(b_re.astype(F32), -1, -2)
    bi = jnp.swapaxes(b_im.astype(F32), -1, -2)
    bbr = cr[..., None, :] * br - ci[..., None, :] * bi
    bbi = cr[..., None, :] * bi + ci[..., None, :] * br
    ccr, cci = c_re.astype(F32), c_im.astype(F32)

    def rows_of(expo, vr, vi):
        pr, pi = apow(expo)
        pr, pi = pr[..., :, None, :], pi[..., :, None, :]
        vr, vi = vr[..., None, :, :], vi[..., None, :, :]
        shape = vr.shape[:3] + (GFLAT, SSM_STATE)
        return (pr * vr - pi * vi).reshape(shape), (pr * vi + pi * vr).reshape(shape)

    k = np.arange(t)
    mrr, mri = rows_of(np.stack([-k, k]), bbr, bbi)
    mcr, mci = rows_of(np.stack([k, -k]), ccr, cci)
    m_dir = (jnp.einsum('ndgrp,ndgqp->ndgrq', mrr, mcr, precision=hp)
             - jnp.einsum('ndgrp,ndgqp->ndgrq', mri, mci, precision=hp))
    s_idx = np.arange(GFLAT)[:, None] // SSM_GROUP
    t_idx = np.arange(GFLAT)[None, :] // SSM_GROUP
    m_g = (jnp.where(t_idx >= s_idx, m_dir[:, 0], 0.0)
           + jnp.where(s_idx >= t_idx, m_dir[:, 1], 0.0))
    srr, sri = rows_of(np.stack([t - 1 - k, k]), bbr, bbi)
    orr, ori = rows_of(np.stack([k + 1, t - k]), ccr, cci)
    atr, ati = apow(t * one)

    def pair_diag(x):
        x = x.reshape(ne, PAIRS, 2, x.shape[-2], x.shape[-1])
        z = jnp.zeros_like(x[:, :, 0])
        top = jnp.concatenate([x[:, :, 0], z], axis=-1)
        bot = jnp.concatenate([z, x[:, :, 1]], axis=-1)
        return jnp.concatenate([top, bot], axis=-2)

    m_pair = pair_diag(m_g)
    s_pair = jnp.concatenate([pair_diag(v[:, d]) for d in range(2) for v in (srr, sri)], axis=-1)
    o_pair = jnp.concatenate([pair_diag(jnp.swapaxes(v[:, d], -1, -2))
                              for d in range(2) for v in (orr, -ori)], axis=-2)
    a_pair = jnp.stack([v[:, d, :, 0, :].reshape(ne, PAIRS, 2 * SSM_STATE)
                        for d in range(2) for v in (atr, ati)], axis=2)
    d_g = d_skip.astype(F32).reshape(ne, PAIRS, 2, 1, SSM_GROUP)
    d_flat = jnp.broadcast_to(d_g, (ne, PAIRS, 2, CHUNK, SSM_GROUP)).reshape(ne, PAIRS, 1, 2 * GFLAT)
    return m_pair.astype(BF16), s_pair.astype(BF16), o_pair.astype(BF16), a_pair, d_flat


def _even_out_kernel(og_ref, yf_ref, gs_ref, h_ref, mod_ref, gw_ref, gb_ref, wo_ref, o_ref, y_scr,
                     acc_scr):
    nb, tt, d = h_ref.shape
    rows = nb * tt
    og = og_ref[...].reshape(rows, ATTN_WIDTH)
    acc_scr[...] = jnp.dot(og, wo_ref[0:ATTN_WIDTH, :], preferred_element_type=F32)
    for kk in range(tt // CHUNK):
        for j in range(SSM_WIDTH // LANES):
            for hh in range(CHUNK // PIECES):
                v = []
                for qi in range(PIECES):
                    col = (j * PIECES + qi) * GFLAT + hh * LANES
                    v.append(pltpu.bitcast(yf_ref[kk * nb:(kk + 1) * nb, col:col + LANES], jnp.uint32))
                w = _piece_transpose(v)
                for i in range(PIECES):
                    tok = kk * CHUNK + hh * PIECES + i
                    y_scr[j, pl.ds(tok, nb, stride=tt), :] = pltpu.bitcast(w[i], BF16).astype(F32)
    y = jnp.concatenate([y_scr[j] for j in range(SSM_WIDTH // LANES)], axis=1)
    z = _gelu_tanh(y)
    t = jnp.dot(z.astype(BF16), gw_ref[...], preferred_element_type=F32) + gb_ref[...]
    gs = gs_ref[...].reshape(rows, SSM_WIDTH).astype(F32)
    os_ = z * _sigmoid(t) * _silu(gs)
    yo = acc_scr[...] + jnp.dot(os_.astype(BF16), wo_ref[ATTN_WIDTH:, :], preferred_element_type=F32)
    o_ref[...] = h_ref[...] + mod_ref[:, 2:3, :] * yo.reshape(nb, tt, d)


def _even_out_call(og, yflat, gs, h, mod, gw, gb, wo, name="even_out"):
    b, t, d = h.shape
    tt = TOK_BLK
    bm = mod.shape[0]
    tok = lambda w: pl.BlockSpec((b, tt, w), lambda ti: (0, ti, 0))
    full = lambda r, c: pl.BlockSpec((r, c), lambda ti: (0, 0))
    return pl.pallas_call(
        _even_out_kernel,
        grid=(t // tt,),
        in_specs=[tok(ATTN_WIDTH),
                  pl.BlockSpec(((tt // CHUNK) * b, FLAT_W), lambda ti: (ti, 0)),
                  tok(SSM_WIDTH), tok(d),
                  pl.BlockSpec((bm, 3, d), lambda ti: (0, 0, 0)),
                  full(SSM_WIDTH, SSM_WIDTH), full(1, SSM_WIDTH), full(d, d)],
        out_specs=tok(d),
        out_shape=jax.ShapeDtypeStruct((b, t, d), F32),
        scratch_shapes=[pltpu.VMEM((SSM_WIDTH // LANES, b * tt, LANES), F32),
                        pltpu.VMEM((b * tt, d), F32)],
        compiler_params=_cparams(("parallel",)),
        name=name,
    )(og, yflat, gs, h, mod, gw, gb.reshape(1, SSM_WIDTH), wo)


def _window_sum(x, r):
    n = x.shape[0]
    up = lambda v, k: pltpu.roll(v, n - k, 0)
    acc, span = x, 1
    while span * 2 <= r * 2:
        acc = acc + up(acc, span)
        span *= 2
    return pltpu.roll(acc, r, 0) + up(x, r)


def _odd_kernel(x_ref, xp_ref, xn_ref, mod_ref, g_ref, wi_ref, pw_ref, ps_ref, wo_ref, fg_ref,
                o_ref, *, tm, seq_len, final):
    ti = pl.program_id(1)
    nt = pl.num_programs(1)
    d = x_ref.shape[-1]
    x = x_ref[0]
    x_ext = jnp.concatenate([xp_ref[0], x, xn_ref[0]], axis=0)
    r = lax.rsqrt(jnp.mean(x_ext * x_ext, axis=-1, keepdims=True) + EPS)
    a_ext = (x_ext * r * g_ref[...]) * (1.0 + mod_ref[0, 1:2, :]) + mod_ref[0, 0:1, :]
    ab = a_ext.astype(BF16)
    u_ext = jnp.dot(ab, wi_ref[:, 0:d], preferred_element_type=F32)
    row = lax.broadcasted_iota(jnp.int32, (tm + 2 * POOL_HALO, 1), 0)
    inside = jnp.logical_and(jnp.logical_or(ti > 0, row >= POOL_HALO),
                             jnp.logical_or(ti < nt - 1, row < POOL_HALO + tm))
    u_ext = jnp.where(inside, u_ext, 0.0)
    gate = jnp.dot(ab[POOL_HALO:POOL_HALO + tm], wi_ref[:, d:2 * d], preferred_element_type=F32)
    u = u_ext[POOL_HALO:POOL_HALO + tm]
    pos = ti * tm + lax.broadcasted_iota(jnp.int32, (tm, 1), 0)
    parts = []
    for gi, w in enumerate(POOL_WINDOWS):
        rad = w // 2
        lanes = slice(gi * POOL_GROUP, (gi + 1) * POOL_GROUP)
        ws = _window_sum(u_ext[:, lanes], rad)[POOL_HALO:POOL_HALO + tm]
        cnt = (jnp.minimum(pos + rad + 1, seq_len) - jnp.maximum(pos - rad, 0)).astype(F32)
        p = ws * (1.0 / cnt) - u[:, lanes]
        parts.append(jnp.dot(p.astype(BF16), pw_ref[gi], preferred_element_type=F32))
    p_all = jnp.concatenate(parts, axis=1) * ps_ref[...]
    mixed = p_all * _silu(gate)
    yo = jnp.dot(mixed.astype(BF16), wo_ref[...], preferred_element_type=F32)
    hn = x + mod_ref[0, 2:3, :] * yo
    if final:
        rr = lax.rsqrt(jnp.mean(hn * hn, axis=-1, keepdims=True) + EPS)
        hn = hn * rr * fg_ref[...]
    o_ref[0] = hn


def _odd_call(h, mod, g, w_in, pool_w, pool_scale, wo, final_g, tm, final, name="odd"):
    b, t, d = h.shape
    hb = tm // POOL_HALO
    nhalo = t // POOL_HALO
    mod_map = (lambda bi, ti: (bi, 0, 0)) if mod.shape[0] == b else (lambda bi, ti: (0, 0, 0))
    tok = pl.BlockSpec((1, tm, d), lambda bi, ti: (bi, ti, 0))
    vec = pl.BlockSpec((1, d), lambda bi, ti: (0, 0))
    return pl.pallas_call(
        functools.partial(_odd_kernel, tm=tm, seq_len=t, final=final),
        grid=(b, t // tm),
        in_specs=[
            tok,
            pl.BlockSpec((1, POOL_HALO, d), lambda bi, ti: (bi, jnp.maximum(ti * hb - 1, 0), 0)),
            pl.BlockSpec((1, POOL_HALO, d), lambda bi, ti: (bi, jnp.minimum((ti + 1) * hb, nhalo - 1), 0)),
            pl.BlockSpec((1, 3, d), mod_map),
            vec,
            pl.BlockSpec((d, 2 * d), lambda bi, ti: (0, 0)),
            pl.BlockSpec((len(POOL_WINDOWS), POOL_GROUP, POOL_GROUP), lambda bi, ti: (0, 0, 0)),
            vec,
            pl.BlockSpec((d, d), lambda bi, ti: (0, 0)),
            vec,
        ],
        out_specs=tok,
        out_shape=jax.ShapeDtypeStruct((b, t, d), F32),
        compiler_params=_cparams(("parallel", "arbitrary")),
        name=name,
    )(h, h, h, mod, g.reshape(1, d), w_in, pool_w, pool_scale.reshape(1, d), wo, final_g.reshape(1, d))


def _rope_tables(n_tokens):
    pos = np.arange(n_tokens)
    row = (pos // GRID_W).astype(np.float32)
    col = (pos % GRID_W).astype(np.float32)
    inv_freq = ROPE_BASE ** (-jnp.arange(ROPE_FREQS, dtype=F32) / ROPE_FREQS)
    ang_r = jnp.asarray(row)[:, None] * inv_freq
    ang_c = jnp.asarray(col)[:, None] * inv_freq
    cos_h = jnp.concatenate([jnp.cos(ang_r)] * 2 + [jnp.cos(ang_c)] * 2, axis=1)
    sin_h = jnp.concatenate([-jnp.sin(ang_r), jnp.sin(ang_r), -jnp.sin(ang_c), jnp.sin(ang_c)], axis=1)
    return jnp.tile(cos_h, (1, 2)), jnp.tile(sin_h, (1, 2))


def _even_weights(w_in):
    c = np.cumsum([0, ATTN_WIDTH, KV_WIDTH, KV_WIDTH, ATTN_WIDTH, SSM_WIDTH, SSM_WIDTH])
    wq = w_in[:, c[0]:c[1]]
    wk = w_in[:, c[1]:c[2]]
    wv = w_in[:, c[2]:c[3]]
    dup = lambda w: jnp.concatenate([w[:, :HEAD_DIM]] * 2 + [w[:, HEAD_DIM:]] * 2, axis=1)
    w = jnp.concatenate([wq, dup(wk), dup(wv), w_in[:, c[3]:]], axis=1)
    return w.astype(BF16)


def kernel(x, c, ctx, c_ctx, ada_w, ada_b, norm_g, even_w_in, even_w_out, attn_sink,
           ssm_a_re, ssm_a_im, ssm_log_dt, ssm_b_re, ssm_b_im, ssm_c_re, ssm_c_im, ssm_d,
           glu_w, glu_b, odd_w_in, odd_w_out, pool_w, pool_scale, final_g):
    b, l, d = x.shape
    lc = ctx.shape[1]
    tm = 512
    rope_tabs = _rope_tables(l)

    s_all = jnp.zeros((24, d), F32).at[:b].set(c).at[b].set(c_ctx)
    mods = _ada_call(s_all, ada_w, ada_b).reshape(DEPTH, 24, 3, d)
    s5_ops = _s5_operators(ssm_a_re, ssm_a_im, ssm_log_dt, ssm_b_re, ssm_b_im,
                           ssm_c_re, ssm_c_im, ssm_d)
    h, hc = x, ctx
    for i in range(DEPTH):
        need_ctx = i < DEPTH - 1
        mod = mods[i, :b]
        mod_c = mods[i, b:b + 1]
        j = i // 2
        if i % 2 == 0:
            w_in = _even_weights(even_w_in[j])
            wo = even_w_out[j].astype(BF16)
            gw = glu_w[j].astype(BF16)
            sink = attn_sink[j].astype(F32)
            q, k4, v4, ga, zl, gs = _even_in_call(h, mod, norm_g[i], w_in, rope_tabs, name="even_in")
            qc, kc4, vc4, gac, zc, gsc = _even_in_call(hc, mod_c, norm_g[i], w_in, None,
                                                       name="even_in_ctx")
            og = _attn_call(sink, q, kc4, vc4, ga, k4, v4, name="attn")
            yc_f, yl_f = _s5_call(zc, zl, *[op[j] for op in s5_ops])
            h_new = _even_out_call(og, yl_f, gs, h, mod, gw, glu_b[j], wo, name="even_out")
            if need_ctx:
                ogc = _attn_call(sink, qc, kc4, vc4, gac, name="attn_ctx")
                hc = _even_out_call(ogc, yc_f, gsc, hc, mod_c, gw, glu_b[j], wo, name="even_out_ctx")
            h = h_new
        else:
            w_in = odd_w_in[j].astype(BF16)
            wo = odd_w_out[j].astype(BF16)
            pw = pool_w[j].astype(BF16)
            final = i == DEPTH - 1
            h_new = _odd_call(h, mod, norm_g[i], w_in, pw, pool_scale[j], wo, final_g, tm, final,
                              name="odd")
            if need_ctx:
                hc = _odd_call(hc, mod_c, norm_g[i], w_in, pw, pool_scale[j], wo, final_g, lc, False,
                               name="odd_ctx")
            h = h_new
    return h
```

```python
import functools
import math

import numpy as np
import jax
import jax.numpy as jnp
from jax import lax
from jax.experimental import pallas as pl
from jax.experimental.pallas import tpu as pltpu

D_MODEL = 1024
DEPTH = 4
GRID_W = 64
EPS = 1e-6
NEG_INF = -1e30
LOG2E = math.log2(math.e)

HEAD_DIM = 64
ATTN_HEADS = 8
KV_HEADS = 2
ATTN_GROUP = ATTN_HEADS // KV_HEADS
ATTN_WIDTH = ATTN_HEADS * HEAD_DIM
KV_WIDTH = KV_HEADS * HEAD_DIM
WINDOW = 128
BLOCK = 128
ROPE_BASE = 10000.0
ROPE_FREQS = HEAD_DIM // 4

SSM_WIDTH = 512
SSM_GROUP = 16
SSM_GROUPS = 32
SSM_STATE = 64
CHUNK = 16
GFLAT = CHUNK * SSM_GROUP
PAIRS = SSM_GROUPS // 2
FLAT_W = SSM_GROUPS * GFLAT

POOL_WINDOWS = (2, 4, 8, 16)
POOL_GROUP = D_MODEL // len(POOL_WINDOWS)
POOL_HALO = 8

LANES = 128
PIECES = LANES // SSM_GROUP
VMEM_LIMIT = 56 * 1024 * 1024
ATTN_TQ = 4096
TOK_BLK = 64

F32 = jnp.float32
BF16 = jnp.bfloat16
YFLAT_DTYPE = BF16


def _sigmoid(x):
    return 0.5 * jnp.tanh(0.5 * x) + 0.5


def _silu(x):
    return x * _sigmoid(x)


def _gelu_tanh(x):
    c = math.sqrt(2.0 / math.pi)
    return 0.5 * x * (1.0 + jnp.tanh(c * (x + 0.044715 * (x * x * x))))


def _cparams(sem):
    return pltpu.CompilerParams(dimension_semantics=sem, vmem_limit_bytes=VMEM_LIMIT)


def _ada_kernel(s_ref, w_ref, b_ref, o_ref):
    s = _silu(s_ref[...]).astype(BF16)
    y = jnp.dot(s, w_ref[0].astype(BF16), preferred_element_type=F32)
    o_ref[0] = y + b_ref[0]


def _ada_call(s_all, ada_w, ada_b):
    rows = s_all.shape[0]
    n = ada_w.shape[-1]
    tn = D_MODEL
    return pl.pallas_call(
        _ada_kernel,
        grid=(DEPTH, n // tn),
        in_specs=[
            pl.BlockSpec((rows, D_MODEL), lambda i, j: (0, 0)),
            pl.BlockSpec((1, D_MODEL, tn), lambda i, j: (i, 0, j)),
            pl.BlockSpec((1, 1, tn), lambda i, j: (i, 0, j)),
        ],
        out_specs=pl.BlockSpec((1, rows, tn), lambda i, j: (i, 0, j)),
        out_shape=jax.ShapeDtypeStruct((DEPTH, rows, n), F32),
        compiler_params=_cparams(("arbitrary", "arbitrary")),
        name="ada_mod",
    )(s_all, ada_w, ada_b.reshape(DEPTH, 1, n))


def _rope(y, cos, sin_signed):
    lane = lax.broadcasted_iota(jnp.int32, (1, LANES), 1)
    first_half = (lane % (2 * ROPE_FREQS)) < ROPE_FREQS
    outs = []
    for j in range(y.shape[1] // LANES):
        yj = y[:, j * LANES:(j + 1) * LANES]
        up = pltpu.roll(yj, LANES - ROPE_FREQS, 1)
        dn = pltpu.roll(yj, ROPE_FREQS, 1)
        partner = jnp.where(first_half, up, dn)
        outs.append(yj * cos + partner * sin_signed)
    return jnp.concatenate(outs, axis=1)


def _piece_transpose(v):
    piece = lax.broadcasted_iota(jnp.int32, (1, LANES), 1) // SSM_GROUP
    v = list(v)
    for dist in (4, 2, 1):
        keep = (piece & dist) == 0
        nv = list(v)
        for i in range(PIECES):
            if i & dist == 0:
                a, b = v[i], v[i + dist]
                nv[i] = jnp.where(keep, a, pltpu.roll(b, dist * SSM_GROUP, 1))
                nv[i + dist] = jnp.where(keep, pltpu.roll(a, LANES - dist * SSM_GROUP, 1), b)
        v = nv
    return v


def _norm_mod(x_ref, mod_ref, g_ref):
    x = x_ref[...]
    r = lax.rsqrt(jnp.mean(x * x, axis=-1, keepdims=True) + EPS)
    shift = mod_ref[:, 0:1, :]
    scale = mod_ref[:, 1:2, :]
    return (x * r * g_ref[...]) * (1.0 + scale) + shift


EV_Q, EV_K, EV_V, EV_GA, EV_U, EV_GS = 0, 512, 768, 1024, 1536, 2048
EV_N = 2560


def _even_in_kernel(*refs, use_rope):
    if use_rope:
        x_ref, mod_ref, g_ref, w_ref, cos_ref, sin_ref = refs[:6]
        refs = refs[6:]
    else:
        x_ref, mod_ref, g_ref, w_ref = refs[:4]
        refs = refs[4:]
    q_ref, k_ref, v_ref, ga_ref, z_ref, gs_ref, u_scr = refs
    nb, tt, d = x_ref.shape
    rows = nb * tt
    ab = _norm_mod(x_ref, mod_ref, g_ref).reshape(rows, d).astype(BF16)

    def proj(start, width):
        return jnp.dot(ab, w_ref[:, start:start + width], preferred_element_type=F32)

    def put(o_ref, y):
        o_ref[...] = y.astype(o_ref.dtype).reshape(o_ref.shape)

    u = proj(EV_U, 512)
    for j in range(SSM_WIDTH // LANES):
        u_scr[j] = u[:, j * LANES:(j + 1) * LANES]
    q = proj(EV_Q, 512)
    k = proj(EV_K, 256)
    for kk in range(tt // CHUNK):
        for j in range(SSM_WIDTH // LANES):
            for hh in range(CHUNK // PIECES):
                w = _piece_transpose(
                    [pltpu.bitcast(
                        u_scr[j, pl.ds(kk * CHUNK + hh * PIECES + i, nb, stride=tt), :].astype(BF16),
                        jnp.uint32) for i in range(PIECES)])
                for qi in range(PIECES):
                    g = j * PIECES + qi
                    col = g * GFLAT + hh * LANES
                    z_ref[kk * nb:(kk + 1) * nb, col:col + LANES] = pltpu.bitcast(w[qi], BF16)
    if use_rope:
        cos = jnp.concatenate([cos_ref[...]] * nb, axis=0)
        sin = jnp.concatenate([sin_ref[...]] * nb, axis=0)
        q = _rope(q, cos, sin)
        k = _rope(k, cos, sin)
    put(q_ref, q * (HEAD_DIM ** -0.5 * LOG2E))
    put(k_ref, k)
    put(v_ref, proj(EV_V, 256))
    put(ga_ref, proj(EV_GA, 512))
    put(gs_ref, proj(EV_GS, 512))


def _even_in_call(x, mods, li, mrow, g_all, w_all, lj, rope_tabs=None, name="even_in"):
    b, t, d = x.shape
    tt = TOK_BLK
    use_rope = rope_tabs is not None
    bm = b if mrow == 0 else 1
    tok = lambda wd: pl.BlockSpec((b, tt, wd), lambda ti: (0, ti, 0))
    in_specs = [
        tok(d),
        pl.BlockSpec((None, bm, 3, d), lambda ti: (li, mrow // bm, 0, 0)),
        pl.BlockSpec((None, 1, d), lambda ti: (li, 0, 0)),
        pl.BlockSpec((None, d, EV_N), lambda ti: (lj, 0, 0)),
    ]
    args = [x, mods, g_all, w_all]
    if use_rope:
        in_specs += [pl.BlockSpec((tt, LANES), lambda ti: (ti, 0))] * 2
        args += list(rope_tabs)
    flat_rows = (tt // CHUNK) * b
    widths = (512, 256, 256, 512, None, 512)
    out_specs, out_shape = [], []
    for wd in widths:
        if wd is None:
            out_specs.append(pl.BlockSpec((flat_rows, FLAT_W), lambda ti: (ti, 0)))
            out_shape.append(jax.ShapeDtypeStruct(((t // CHUNK) * b, FLAT_W), BF16))
        else:
            out_specs.append(tok(wd))
            out_shape.append(jax.ShapeDtypeStruct((b, t, wd), BF16))
    return pl.pallas_call(
        functools.partial(_even_in_kernel, use_rope=use_rope),
        grid=(t // tt,),
        in_specs=in_specs,
        out_specs=out_specs,
        out_shape=out_shape,
        scratch_shapes=[pltpu.VMEM((SSM_WIDTH // LANES, b * tt, LANES), F32)],
        compiler_params=_cparams(("parallel",)),
        name=name,
    )(*args)


def _attn_scores(q_ref, kc_ref, k_ref, bias_ref, r0, q0, seq_len, heads=tuple(range(KV_HEADS))):
    pair_w = ATTN_GROUP * HEAD_DIM
    head_of_lane = lax.broadcasted_iota(jnp.int32, (1, pair_w), 1) // HEAD_DIM
    nt_dims = (((1,), (1,)), ((), ()))
    local = k_ref is not None
    if local:
        start = pl.multiple_of(jnp.clip(q0 - BLOCK, 0, seq_len - 3 * BLOCK), BLOCK)
        variant = jnp.where(q0 == 0, 0, jnp.where(q0 == seq_len - BLOCK, 2, 1))
        bias1 = bias_ref[variant]
        bias = jnp.concatenate([bias1] * ATTN_GROUP, axis=0)
    out = []
    for hk in heads:
        lanes = slice(hk * pair_w, (hk + 1) * pair_w)
        kv_lanes = slice(hk * 2 * HEAD_DIM, (hk + 1) * 2 * HEAD_DIM)
        qb = q_ref[0, pl.ds(r0, BLOCK), lanes]
        zero = jnp.zeros_like(qb)
        qs = jnp.concatenate(
            [jnp.where(head_of_lane == g, qb, zero) for g in range(ATTN_GROUP)], axis=0)
        kc = kc_ref[0, :, kv_lanes]
        kc4 = jnp.concatenate([kc, kc], axis=1)
        s_ctx = lax.dot_general(qs, kc4, nt_dims, preferred_element_type=F32)
        if local:
            kl = k_ref[0, pl.ds(start, 3 * BLOCK), kv_lanes]
            kl4 = jnp.concatenate([kl, kl], axis=1)
            s_loc = lax.dot_general(qs, kl4, nt_dims, preferred_element_type=F32) + bias
            out.append((s_ctx, s_loc))
        else:
            out.append((s_ctx,))
    return out


def _attn_finish(scores, sink_ref, lj, vc_ref, v_ref, ga_ref, o_ref, r0, q0, seq_len,
                 heads=tuple(range(KV_HEADS))):
    pair_w = ATTN_GROUP * HEAD_DIM
    head_of_lane = lax.broadcasted_iota(jnp.int32, (1, pair_w), 1) // HEAD_DIM
    local = v_ref is not None
    if local:
        start = pl.multiple_of(jnp.clip(q0 - BLOCK, 0, seq_len - 3 * BLOCK), BLOCK)
    for n, hk in enumerate(heads):
        lanes = slice(hk * pair_w, (hk + 1) * pair_w)
        kv_lanes = slice(hk * 2 * HEAD_DIM, (hk + 1) * 2 * HEAD_DIM)
        parts = scores[n]
        tiles = [p[:, t * LANES:(t + 1) * LANES] for p in parts for t in range(p.shape[1] // LANES)]
        sink_col = jnp.concatenate(
            [jnp.full((BLOCK, 1), sink_ref[lj, hk * ATTN_GROUP + g] * LOG2E, F32)
             for g in range(ATTN_GROUP)], axis=0)
        mt = tiles[0]
        for tl in tiles[1:]:
            mt = jnp.maximum(mt, tl)
        m = jnp.maximum(jnp.max(mt, axis=-1, keepdims=True), sink_col)
        es = [jnp.exp2(p - m) for p in parts]
        et = None
        for e in es:
            for t in range(e.shape[1] // LANES):
                tl = e[:, t * LANES:(t + 1) * LANES]
                et = tl if et is None else et + tl
        den = jnp.sum(et, axis=-1, keepdims=True) + jnp.exp2(sink_col - m)
        vc = vc_ref[0, :, kv_lanes]
        vc4 = jnp.concatenate([vc, vc], axis=1)
        o = jnp.dot(es[0].astype(BF16), vc4, preferred_element_type=F32)
        if local:
            vl = v_ref[0, pl.ds(start, 3 * BLOCK), kv_lanes]
            vl4 = jnp.concatenate([vl, vl], axis=1)
            o = o + jnp.dot(es[1].astype(BF16), vl4, preferred_element_type=F32)
        o = o * (1.0 / den)
        oh = jnp.zeros((BLOCK, pair_w), F32)
        for g in range(ATTN_GROUP):
            oh = oh + jnp.where(head_of_lane == g, o[g * BLOCK:(g + 1) * BLOCK, :], 0.0)
        gate = ga_ref[0, pl.ds(r0, BLOCK), lanes].astype(F32)
        o_ref[0, pl.ds(r0, BLOCK), lanes] = (oh * _silu(gate)).astype(o_ref.dtype)


def _attn_kernel(sink_ref, q_ref, kc_ref, vc_ref, ga_ref, bias_ref, k_ref, v_ref, o_ref,
                 sc_scr, sl_scr, *, tq, seq_len, lj):
    qi = pl.program_id(1)
    nblk = tq // BLOCK

    def scores_to(i, slot, heads):
        r0 = pl.multiple_of(i * BLOCK, BLOCK)
        sc = _attn_scores(q_ref, kc_ref, k_ref, bias_ref, r0, qi * tq + i * BLOCK, seq_len, heads)
        for n, hk in enumerate(heads):
            sc_scr[slot, hk] = sc[n][0]
            sl_scr[slot, hk] = sc[n][1]

    def finish_from(i, slot, heads):
        r0 = pl.multiple_of(i * BLOCK, BLOCK)
        sc = [(sc_scr[slot, hk], sl_scr[slot, hk]) for hk in heads]
        _attn_finish(sc, sink_ref, lj, vc_ref, v_ref, ga_ref, o_ref, r0, qi * tq + i * BLOCK, seq_len,
                     heads)

    both = tuple(range(KV_HEADS))
    scores_to(0, 0, both)

    def body(j, carry):
        i = 2 * j
        for hk in both:
            scores_to(i + 1, 1, (hk,))
            finish_from(i, 0, (hk,))
        for hk in both:
            scores_to(i + 2, 0, (hk,))
            finish_from(i + 1, 1, (hk,))
        return carry

    lax.fori_loop(0, nblk // 2 - 1, body, 0)
    for hk in both:
        scores_to(nblk - 1, 1, (hk,))
        finish_from(nblk - 2, 0, (hk,))
    finish_from(nblk - 1, 1, both)


def _attn_ctx_kernel(sink_ref, q_ref, kc_ref, vc_ref, ga_ref, o_ref, *, tq, lj):
    for i in range(tq // BLOCK):
        sc = _attn_scores(q_ref, kc_ref, None, None, i * BLOCK, 0, 0)
        _attn_finish(sc, sink_ref, lj, vc_ref, None, ga_ref, o_ref, i * BLOCK, 0, 0)


def _band_bias():
    i = np.arange(BLOCK)[:, None]
    j = np.arange(3 * BLOCK)[None, :]
    tabs = []
    for off in (0, -BLOCK, -2 * BLOCK):
        valid = np.abs(i - (j + off)) <= WINDOW
        tabs.append(np.where(valid, 0.0, NEG_INF))
    return jnp.asarray(np.stack(tabs), F32)


def _attn_call(sink, lj, q, kc4, vc4, ga, k4=None, v4=None, tq=ATTN_TQ, name="attn"):
    b, t, _ = q.shape
    lc = kc4.shape[1]
    local = k4 is not None
    tq = tq if local else t
    in_specs = [
        pl.BlockSpec(memory_space=pltpu.SMEM),
        pl.BlockSpec((1, tq, ATTN_WIDTH), lambda bi, ti: (bi, ti, 0)),
        pl.BlockSpec((1, lc, 2 * KV_WIDTH), lambda bi, ti: (bi, 0, 0)),
        pl.BlockSpec((1, lc, 2 * KV_WIDTH), lambda bi, ti: (bi, 0, 0)),
        pl.BlockSpec((1, tq, ATTN_WIDTH), lambda bi, ti: (bi, ti, 0)),
    ]
    args = [sink, q, kc4, vc4, ga]
    scratch = []
    if local:
        in_specs += [pl.BlockSpec((3, BLOCK, 3 * BLOCK), lambda bi, ti: (0, 0, 0))]
        in_specs += [pl.BlockSpec((1, t, 2 * KV_WIDTH), lambda bi, ti: (bi, 0, 0))] * 2
        args += [_band_bias(), k4, v4]
        rows = ATTN_GROUP * BLOCK
        scratch = [pltpu.VMEM((2, KV_HEADS, rows, lc), F32),
                   pltpu.VMEM((2, KV_HEADS, rows, 3 * BLOCK), F32)]
        body = functools.partial(_attn_kernel, tq=tq, seq_len=t, lj=lj)
    else:
        body = functools.partial(_attn_ctx_kernel, tq=tq, lj=lj)
    return pl.pallas_call(
        body,
        grid=(b, t // tq),
        in_specs=in_specs,
        out_specs=pl.BlockSpec((1, tq, ATTN_WIDTH), lambda bi, ti: (bi, ti, 0)),
        out_shape=jax.ShapeDtypeStruct((b, t, ATTN_WIDTH), BF16),
        scratch_shapes=scratch,
        compiler_params=_cparams(("parallel", "arbitrary")),
        name=name,
    )(*args)


S5_TILE = 512
CROWS = 16


def _s5_kernel(zc_ref, zl_ref, m_ref, s_ref, o_ref, a_ref, d_ref, yc_ref, yl_ref,
               hc_buf, hl_buf, sc_buf, sl_buf):
    half = 2 * SSM_STATE
    nc_rows = zc_ref.shape[0]
    nl_rows = zl_ref.shape[0]

    sc_buf[...] = jnp.dot(zc_ref[...], s_ref[0], preferred_element_type=F32)
    for t in range(nl_rows // S5_TILE):
        rows = pl.ds(t * S5_TILE, S5_TILE)
        sl_buf[rows, :] = jnp.dot(zl_ref[rows, :], s_ref[0], preferred_element_type=F32)

    a_f = (a_ref[0, 0:1, :], a_ref[0, 1:2, :])
    a_b = (a_ref[0, 2:3, :], a_ref[0, 3:4, :])

    def scan(s_buf, h_buf, nrows, carry):
        nchunks = nrows // CROWS

        def one(a, h, s_buf, h_buf, row, lane0):
            h_re, h_im = h
            h_buf[pl.ds(row, CROWS), lane0:lane0 + 2 * half] = (
                jnp.concatenate([h_re, h_im], axis=1).astype(BF16))
            s_re = s_buf[pl.ds(row, CROWS), lane0:lane0 + half]
            s_im = s_buf[pl.ds(row, CROWS), lane0 + half:lane0 + 2 * half]
            return (a[0] * h_re - a[1] * h_im + s_re, a[0] * h_im + a[1] * h_re + s_im)

        def step(i, c):
            hf, hb = c
            rf = pl.multiple_of(i * CROWS, CROWS)
            rb = pl.multiple_of((nchunks - 1 - i) * CROWS, CROWS)
            return (one(a_f, hf, s_buf, h_buf, rf, 0), one(a_b, hb, s_buf, h_buf, rb, 2 * half))

        return lax.fori_loop(0, nchunks, step, carry, unroll=2)

    zero = jnp.zeros((CROWS, half), F32)
    carry = scan(sc_buf, hc_buf, nc_rows, ((zero, zero), (zero, zero)))
    scan(sl_buf, hl_buf, nl_rows, carry)

    def emit(z_ref, h_buf, y_ref, row0, nrows):
        z = z_ref[pl.ds(row0, nrows), :]
        y = jnp.concatenate(
            [jnp.dot(z[:, gi * GFLAT:(gi + 1) * GFLAT], m_ref[0, gi], preferred_element_type=F32)
             for gi in range(2)], axis=1)
        y = y + jnp.dot(h_buf[pl.ds(row0, nrows), :], o_ref[0], preferred_element_type=F32)
        y = y + z.astype(F32) * d_ref[0]
        y_ref[pl.ds(row0, nrows), :] = y.astype(y_ref.dtype)

    emit(zc_ref, hc_buf, yc_ref, 0, nc_rows)
    for t in range(nl_rows // S5_TILE):
        emit(zl_ref, hl_buf, yl_ref, t * S5_TILE, S5_TILE)


def _s5_call(zc, zl, m_op, s_op, o_op, a_op, d_op, lj):
    rc, rl = zc.shape[0], zl.shape[0]
    pw = 2 * GFLAT
    row_spec = lambda r: pl.BlockSpec((r, pw), lambda p: (0, p))
    op_spec = pl.BlockSpec((None, 1, pw, pw), lambda p: (lj, p, 0, 0))
    return pl.pallas_call(
        _s5_kernel,
        grid=(PAIRS,),
        in_specs=[
            row_spec(rc), row_spec(rl),
            pl.BlockSpec((None, 1, 2, GFLAT, GFLAT), lambda p: (lj, p, 0, 0, 0)), op_spec, op_spec,
            pl.BlockSpec((None, 1, 4, 2 * SSM_STATE), lambda p: (lj, p, 0, 0)),
            pl.BlockSpec((None, 1, 1, pw), lambda p: (lj, p, 0, 0)),
        ],
        out_specs=[row_spec(rc), row_spec(rl)],
        out_shape=[jax.ShapeDtypeStruct(zc.shape, YFLAT_DTYPE), jax.ShapeDtypeStruct(zl.shape, YFLAT_DTYPE)],
        scratch_shapes=[
            pltpu.VMEM((rc, pw), BF16),
            pltpu.VMEM((rl, pw), BF16),
            pltpu.VMEM((rc, pw), F32),
            pltpu.VMEM((rl, pw), F32),
        ],
        compiler_params=_cparams(("parallel",)),
        name="s5_scan",
    )(zc, zl, m_op, s_op, o_op, a_op, d_op)


def _s5_op_kernel(lr_ref, li_ref, bbr_ref, bbi_ref, ccr_ref, cci_ref, m_ref, s_ref, o_ref):
    t = CHUNK
    hp = lax.Precision.HIGHEST
    nt_dims = (((1,), (1,)), ((), ()))
    kk = lax.broadcasted_iota(jnp.int32, (t, 1), 0).astype(F32)
    s_idx = lax.broadcasted_iota(jnp.int32, (GFLAT, GFLAT), 0) // SSM_GROUP
    t_idx = lax.broadcasted_iota(jnp.int32, (GFLAT, GFLAT), 1) // SSM_GROUP
    half = 2 * SSM_STATE

    def rep_k(v):
        return jnp.concatenate(
            [jnp.broadcast_to(v[k:k + 1, :], (SSM_GROUP, SSM_STATE)) for k in range(t)], axis=0)

    def rep_c(v):
        return jnp.concatenate([v] * t, axis=0)

    s_ref[...] = jnp.zeros(s_ref.shape, s_ref.dtype)
    o_ref[...] = jnp.zeros(o_ref.shape, o_ref.dtype)
    m_acc = [None, None]
    for q in range(4):
        direction, gi = q // 2, q % 2
        lr, li = lr_ref[q], li_ref[q]
        bb = (bbr_ref[q], bbi_ref[q])
        cc = (ccr_ref[q], cci_ref[q])

        def rows_of(expo, v):
            mag = jnp.exp(lr * expo)
            pr, pi = mag * jnp.cos(li * expo), mag * jnp.sin(li * expo)
            pr, pi = rep_k(pr), rep_k(pi)
            vr, vi = rep_c(v[0]), rep_c(v[1])
            return pr * vr - pi * vi, pr * vi + pi * vr

        if direction == 0:
            e_mr, e_mc, e_s, e_o = -kk, kk, (t - 1.0) - kk, kk + 1.0
            keep = t_idx >= s_idx
        else:
            e_mr, e_mc, e_s, e_o = kk, -kk, kk, t - kk
            keep = s_idx >= t_idx
        rr, ri = rows_of(e_mr, bb)
        cr, ci = rows_of(e_mc, cc)
        mq = (lax.dot_general(rr, cr, nt_dims, precision=hp, preferred_element_type=F32)
              - lax.dot_general(ri, ci, nt_dims, precision=hp, preferred_element_type=F32))
        mq = jnp.where(keep, mq, 0.0)
        m_acc[gi] = mq if m_acc[gi] is None else m_acc[gi] + mq
        sr, si = rows_of(e_s, bb)
        rows = slice(gi * GFLAT, (gi + 1) * GFLAT)
        c0 = direction * 2 * half + gi * SSM_STATE
        s_ref[rows, c0:c0 + SSM_STATE] = sr.astype(s_ref.dtype)
        s_ref[rows, c0 + half:c0 + half + SSM_STATE] = si.astype(s_ref.dtype)
        orr, ori = rows_of(e_o, cc)
        o_t, oi_t = orr.T, (-ori).T
        o_ref[c0:c0 + SSM_STATE, rows] = o_t.astype(o_ref.dtype)
        o_ref[c0 + half:c0 + half + SSM_STATE, rows] = oi_t.astype(o_ref.dtype)
    for gi in range(2):
        m_ref[gi] = m_acc[gi].astype(m_ref.dtype)


def _s5_operators(a_re, a_im, log_dt, b_re, b_im, c_re, c_im, d_skip):
    t = CHUNK
    ne = a_re.shape[0]
    ar, ai = a_re.astype(F32), a_im.astype(F32)
    dt = jnp.exp(log_dt.astype(F32))[..., None]
    lr, li = ar * dt, ai * dt
    mag = jnp.exp(lr)
    xr, xi = mag * jnp.cos(li) - 1.0, mag * jnp.sin(li)
    den = ar * ar + ai * ai
    cr, ci = (xr * ar + xi * ai) / den, (xi * ar - xr * ai) / den
    br = jnp.swapaxes(b_re.astype(F32), -1, -2)
    bi = jnp.swapaxes(b_im.astype(F32), -1, -2)
    bbr = cr[..., None, :] * br - ci[..., None, :] * bi
    bbi = cr[..., None, :] * bi + ci[..., None, :] * br

    def per_pair(v):
        tail = v.shape[3:]
        v = v.reshape((ne, 2, PAIRS, 2) + tail)
        v = jnp.moveaxis(v, 1, 2)
        return v.reshape((ne, PAIRS, 4) + tail)

    small = [per_pair(v[..., None, :]) for v in (lr, li)]
    mats = [per_pair(v) for v in (bbr, bbi, c_re.astype(F32), c_im.astype(F32))]
    spec = lambda rws: pl.BlockSpec((None, None, 4, rws, SSM_STATE), lambda n, p: (n, p, 0, 0, 0))
    pw = 2 * GFLAT
    m_pair, s_pair, o_pair = pl.pallas_call(
        _s5_op_kernel,
        grid=(ne, PAIRS),
        in_specs=[spec(1), spec(1)] + [spec(SSM_GROUP)] * 4,
        out_specs=[pl.BlockSpec((None, None, 2, GFLAT, GFLAT), lambda n, p: (n, p, 0, 0, 0)),
                   pl.BlockSpec((None, None, pw, pw), lambda n, p: (n, p, 0, 0)),
                   pl.BlockSpec((None, None, pw, pw), lambda n, p: (n, p, 0, 0))],
        out_shape=[jax.ShapeDtypeStruct((ne, PAIRS, 2, GFLAT, GFLAT), BF16),
                   jax.ShapeDtypeStruct((ne, PAIRS, pw, pw), BF16),
                   jax.ShapeDtypeStruct((ne, PAIRS, pw, pw), BF16)],
        compiler_params=_cparams(("parallel", "parallel")),
        name="s5_ops",
    )(*small, *mats)

    magt = jnp.exp(lr * t)
    atr, ati = magt * jnp.cos(li * t), magt * jnp.sin(li * t)
    a_pair = jnp.stack([v[:, d].reshape(ne, PAIRS, 2 * SSM_STATE)
                        for d in range(2) for v in (atr, ati)], axis=2)
    d_g = d_skip.astype(F32).reshape(ne, PAIRS, 2, 1, SSM_GROUP)
    d_flat = jnp.broadcast_to(d_g, (ne, PAIRS, 2, CHUNK, SSM_GROUP)).reshape(ne, PAIRS, 1, 2 * GFLAT)
    return m_pair, s_pair, o_pair, a_pair, d_flat


def _even_out_kernel(og_ref, yf_ref, gs_ref, h_ref, mod_ref, gw_ref, gb_ref, wo_ref, o_ref, y_scr,
                     acc_scr):
    nb, tt, d = h_ref.shape
    rows = nb * tt
    og = og_ref[...].reshape(rows, ATTN_WIDTH)
    acc_scr[...] = jnp.dot(og, wo_ref[0:ATTN_WIDTH, :], preferred_element_type=F32)
    for kk in range(tt // CHUNK):
        for j in range(SSM_WIDTH // LANES):
            for hh in range(CHUNK // PIECES):
                v = []
                for qi in range(PIECES):
                    col = (j * PIECES + qi) * GFLAT + hh * LANES
                    v.append(pltpu.bitcast(yf_ref[kk * nb:(kk + 1) * nb, col:col + LANES], jnp.uint32))
                w = _piece_transpose(v)
                for i in range(PIECES):
                    tok = kk * CHUNK + hh * PIECES + i
                    y_scr[j, pl.ds(tok, nb, stride=tt), :] = pltpu.bitcast(w[i], BF16).astype(F32)
    y = jnp.concatenate([y_scr[j] for j in range(SSM_WIDTH // LANES)], axis=1)
    z = _gelu_tanh(y)
    t = jnp.dot(z.astype(BF16), gw_ref[...], preferred_element_type=F32) + gb_ref[...]
    gs = gs_ref[...].reshape(rows, SSM_WIDTH).astype(F32)
    os_ = z * _sigmoid(t) * _silu(gs)
    yo = acc_scr[...] + jnp.dot(os_.astype(BF16), wo_ref[ATTN_WIDTH:, :], preferred_element_type=F32)
    o_ref[...] = h_ref[...] + mod_ref[:, 2:3, :] * yo.reshape(nb, tt, d)


def _even_out_call(og, yflat, gs, h, mods, li, mrow, gw_all, gb_all, wo_all, lj, name="even_out"):
    b, t, d = h.shape
    tt = TOK_BLK
    bm = b if mrow == 0 else 1
    tok = lambda w: pl.BlockSpec((b, tt, w), lambda ti: (0, ti, 0))
    layer = lambda r, c: pl.BlockSpec((None, r, c), lambda ti: (lj, 0, 0))
    return pl.pallas_call(
        _even_out_kernel,
        grid=(t // tt,),
        in_specs=[tok(ATTN_WIDTH),
                  pl.BlockSpec(((tt // CHUNK) * b, FLAT_W), lambda ti: (ti, 0)),
                  tok(SSM_WIDTH), tok(d),
                  pl.BlockSpec((None, bm, 3, d), lambda ti: (li, mrow // bm, 0, 0)),
                  layer(SSM_WIDTH, SSM_WIDTH), layer(1, SSM_WIDTH), layer(d, d)],
        out_specs=tok(d),
        out_shape=jax.ShapeDtypeStruct((b, t, d), F32),
        scratch_shapes=[pltpu.VMEM((SSM_WIDTH // LANES, b * tt, LANES), F32),
                        pltpu.VMEM((b * tt, d), F32)],
        compiler_params=_cparams(("parallel",)),
        name=name,
    )(og, yflat, gs, h, mods, gw_all, gb_all, wo_all)


def _window_sum(x, r):
    n = x.shape[0]
    up = lambda v, k: pltpu.roll(v, n - k, 0)
    acc, span = x, 1
    while span * 2 <= r * 2:
        acc = acc + up(acc, span)
        span *= 2
    return pltpu.roll(acc, r, 0) + up(x, r)


def _odd_kernel(x_ref, xp_ref, xn_ref, mod_ref, g_ref, wi_ref, pw_ref, ps_ref, wo_ref, fg_ref,
                o_ref, *, tm, seq_len, final):
    ti = pl.program_id(1)
    nt = pl.num_programs(1)
    d = x_ref.shape[-1]
    x = x_ref[0]
    x_ext = jnp.concatenate([xp_ref[0], x, xn_ref[0]], axis=0)
    r = lax.rsqrt(jnp.mean(x_ext * x_ext, axis=-1, keepdims=True) + EPS)
    a_ext = (x_ext * r * g_ref[...]) * (1.0 + mod_ref[0, 1:2, :]) + mod_ref[0, 0:1, :]
    ab = a_ext.astype(BF16)
    u_ext = jnp.dot(ab, wi_ref[:, 0:d], preferred_element_type=F32)
    row = lax.broadcasted_iota(jnp.int32, (tm + 2 * POOL_HALO, 1), 0)
    inside = jnp.logical_and(jnp.logical_or(ti > 0, row >= POOL_HALO),
                             jnp.logical_or(ti < nt - 1, row < POOL_HALO + tm))
    u_ext = jnp.where(inside, u_ext, 0.0)
    gate = jnp.dot(ab[POOL_HALO:POOL_HALO + tm], wi_ref[:, d:2 * d], preferred_element_type=F32)
    u = u_ext[POOL_HALO:POOL_HALO + tm]
    pos = ti * tm + lax.broadcasted_iota(jnp.int32, (tm, 1), 0)
    parts = []
    for gi, w in enumerate(POOL_WINDOWS):
        rad = w // 2
        lanes = slice(gi * POOL_GROUP, (gi + 1) * POOL_GROUP)
        ws = _window_sum(u_ext[:, lanes], rad)[POOL_HALO:POOL_HALO + tm]
        cnt = (jnp.minimum(pos + rad + 1, seq_len) - jnp.maximum(pos - rad, 0)).astype(F32)
        p = ws * (1.0 / cnt) - u[:, lanes]
        parts.append(jnp.dot(p.astype(BF16), pw_ref[gi], preferred_element_type=F32))
    p_all = jnp.concatenate(parts, axis=1) * ps_ref[...]
    mixed = p_all * _silu(gate)
    yo = jnp.dot(mixed.astype(BF16), wo_ref[...], preferred_element_type=F32)
    hn = x + mod_ref[0, 2:3, :] * yo
    if final:
        rr = lax.rsqrt(jnp.mean(hn * hn, axis=-1, keepdims=True) + EPS)
        hn = hn * rr * fg_ref[...]
    o_ref[0] = hn


def _odd_call(h, mods, li, mrow, g_all, w_in_all, pool_w_all, pool_scale_all, wo_all, lj, final_g,
              tm, final, name="odd"):
    b, t, d = h.shape
    hb = tm // POOL_HALO
    nhalo = t // POOL_HALO
    mod_map = (lambda bi, ti: (li, bi, 0, 0)) if mrow == 0 else (lambda bi, ti: (li, mrow, 0, 0))
    tok = pl.BlockSpec((1, tm, d), lambda bi, ti: (bi, ti, 0))
    npw = len(POOL_WINDOWS)
    return pl.pallas_call(
        functools.partial(_odd_kernel, tm=tm, seq_len=t, final=final),
        grid=(b, t // tm),
        in_specs=[
            tok,
            pl.BlockSpec((1, POOL_HALO, d), lambda bi, ti: (bi, jnp.maximum(ti * hb - 1, 0), 0)),
            pl.BlockSpec((1, POOL_HALO, d), lambda bi, ti: (bi, jnp.minimum((ti + 1) * hb, nhalo - 1), 0)),
            pl.BlockSpec((None, 1, 3, d), mod_map),
            pl.BlockSpec((None, 1, d), lambda bi, ti: (li, 0, 0)),
            pl.BlockSpec((None, d, 2 * d), lambda bi, ti: (lj, 0, 0)),
            pl.BlockSpec((None, npw, POOL_GROUP, POOL_GROUP), lambda bi, ti: (lj, 0, 0, 0)),
            pl.BlockSpec((None, 1, d), lambda bi, ti: (lj, 0, 0)),
            pl.BlockSpec((None, d, d), lambda bi, ti: (lj, 0, 0)),
            pl.BlockSpec((1, d), lambda bi, ti: (0, 0)),
        ],
        out_specs=tok,
        out_shape=jax.ShapeDtypeStruct((b, t, d), F32),
        compiler_params=_cparams(("parallel", "arbitrary")),
        name=name,
    )(h, h, h, mods, g_all, w_in_all, pool_w_all, pool_scale_all, wo_all, final_g)


def _rope_tables(n_tokens):
    pos = np.arange(n_tokens)
    inv_freq = ROPE_BASE ** (-np.arange(ROPE_FREQS, dtype=np.float64) / ROPE_FREQS)
    ang_r = (pos // GRID_W)[:, None] * inv_freq
    ang_c = (pos % GRID_W)[:, None] * inv_freq
    cos_h = np.concatenate([np.cos(ang_r)] * 2 + [np.cos(ang_c)] * 2, axis=1)
    sin_h = np.concatenate([-np.sin(ang_r), np.sin(ang_r), -np.sin(ang_c), np.sin(ang_c)], axis=1)
    return (jnp.asarray(np.tile(cos_h, (1, 2)), F32), jnp.asarray(np.tile(sin_h, (1, 2)), F32))


def _even_weights(w_in):
    c = np.cumsum([0, ATTN_WIDTH, KV_WIDTH, KV_WIDTH, ATTN_WIDTH, SSM_WIDTH, SSM_WIDTH])
    wq = w_in[..., c[0]:c[1]]
    wk = w_in[..., c[1]:c[2]]
    wv = w_in[..., c[2]:c[3]]
    dup = lambda w: jnp.concatenate([w[..., :HEAD_DIM]] * 2 + [w[..., HEAD_DIM:]] * 2, axis=-1)
    w = jnp.concatenate([wq, dup(wk), dup(wv), w_in[..., c[3]:]], axis=-1)
    return w.astype(BF16)


def kernel(x, c, ctx, c_ctx, ada_w, ada_b, norm_g, even_w_in, even_w_out, attn_sink,
           ssm_a_re, ssm_a_im, ssm_log_dt, ssm_b_re, ssm_b_im, ssm_c_re, ssm_c_im, ssm_d,
           glu_w, glu_b, odd_w_in, odd_w_out, pool_w, pool_scale, final_g):
    b, l, d = x.shape
    lc = ctx.shape[1]
    tm = 512
    rope_tabs = _rope_tables(l)

    s_all = jnp.concatenate([c, c_ctx[None, :], jnp.zeros((24 - b - 1, d), F32)], axis=0)
    mods = _ada_call(s_all, ada_w, ada_b).reshape(DEPTH, 24, 3, d)
    s5_ops = _s5_operators(ssm_a_re, ssm_a_im, ssm_log_dt, ssm_b_re, ssm_b_im,
                           ssm_c_re, ssm_c_im, ssm_d)
    g_all = norm_g.reshape(DEPTH, 1, d)
    ew_in, ew_out, gw_all = _even_weights(even_w_in), even_w_out.astype(BF16), glu_w.astype(BF16)
    gb_all = glu_b.reshape(-1, 1, SSM_WIDTH)
    ow_in, ow_out, pw_all = odd_w_in.astype(BF16), odd_w_out.astype(BF16), pool_w.astype(BF16)
    ps_all = pool_scale.reshape(-1, 1, d)
    fg = final_g.reshape(1, d)
    sink = attn_sink.astype(F32)
    h, hc = x, ctx
    for i in range(DEPTH):
        need_ctx = i < DEPTH - 1
        j = i // 2
        if i % 2 == 0:
            q, k4, v4, ga, zl, gs = _even_in_call(h, mods, i, 0, g_all, ew_in, j, rope_tabs, name="even_in")
            qc, kc4, vc4, gac, zc, gsc = _even_in_call(hc, mods, i, b, g_all, ew_in, j, None,
                                                       name="even_in_ctx")
            og = _attn_call(sink, j, q, kc4, vc4, ga, k4, v4, name="attn")
            yc_f, yl_f = _s5_call(zc, zl, *s5_ops, j)
            h_new = _even_out_call(og, yl_f, gs, h, mods, i, 0, gw_all, gb_all, ew_out, j, name="even_out")
            if need_ctx:
                ogc = _attn_call(sink, j, qc, kc4, vc4, gac, name="attn_ctx")
                hc = _even_out_call(ogc, yc_f, gsc, hc, mods, i, b, gw_all, gb_all, ew_out, j,
                                    name="even_out_ctx")
            h = h_new
        else:
            final = i == DEPTH - 1
            h_new = _odd_call(h, mods, i, 0, g_all, ow_in, pw_all, ps_all, ow_out, j, fg, tm, final,
                              name="odd")
            if need_ctx:
                hc = _odd_call(hc, mods, i, b, g_all, ow_in, pw_all, ps_all, ow_out, j, fg, lc, False,
                               name="odd_ctx")
            h = h_new
    return h
```

```python
import functools
import math

import numpy as np
import jax
import jax.numpy as jnp
from jax import lax
from jax.experimental import pallas as pl
from jax.experimental.pallas import tpu as pltpu

D_MODEL = 1024
DEPTH = 4
GRID_W = 64
EPS = 1e-6
NEG_INF = -1e30
LOG2E = math.log2(math.e)

HEAD_DIM = 64
ATTN_HEADS = 8
KV_HEADS = 2
ATTN_GROUP = ATTN_HEADS // KV_HEADS
ATTN_WIDTH = ATTN_HEADS * HEAD_DIM
KV_WIDTH = KV_HEADS * HEAD_DIM
WINDOW = 128
BLOCK = 128
ROPE_BASE = 10000.0
ROPE_FREQS = HEAD_DIM // 4

SSM_WIDTH = 512
SSM_GROUP = 16
SSM_GROUPS = 32
SSM_STATE = 64
CHUNK = 16
GFLAT = CHUNK * SSM_GROUP
PAIRS = SSM_GROUPS // 2
FLAT_W = SSM_GROUPS * GFLAT

POOL_WINDOWS = (2, 4, 8, 16)
POOL_GROUP = D_MODEL // len(POOL_WINDOWS)
POOL_HALO = 8

LANES = 128
PIECES = LANES // SSM_GROUP
VMEM_LIMIT = 56 * 1024 * 1024
ATTN_TQ = 4096
TOK_BLK = 64

F32 = jnp.float32
BF16 = jnp.bfloat16
YFLAT_DTYPE = BF16


def _sigmoid(x):
    return 0.5 * jnp.tanh(0.5 * x) + 0.5


def _silu(x):
    return x * _sigmoid(x)


def _gelu_tanh(x):
    c = math.sqrt(2.0 / math.pi)
    return 0.5 * x * (1.0 + jnp.tanh(c * (x + 0.044715 * (x * x * x))))


def _cparams(sem):
    return pltpu.CompilerParams(dimension_semantics=sem, vmem_limit_bytes=VMEM_LIMIT)


def _ada_kernel(s_ref, w_ref, b_ref, o_ref):
    s = _silu(s_ref[...]).astype(BF16)
    y = jnp.dot(s, w_ref[0].astype(BF16), preferred_element_type=F32)
    o_ref[0] = y + b_ref[0]


def _ada_call(s_all, ada_w, ada_b):
    rows = s_all.shape[0]
    n = ada_w.shape[-1]
    tn = D_MODEL
    return pl.pallas_call(
        _ada_kernel,
        grid=(DEPTH, n // tn),
        in_specs=[
            pl.BlockSpec((rows, D_MODEL), lambda i, j: (0, 0)),
            pl.BlockSpec((1, D_MODEL, tn), lambda i, j: (i, 0, j)),
            pl.BlockSpec((1, 1, tn), lambda i, j: (i, 0, j)),
        ],
        out_specs=pl.BlockSpec((1, rows, tn), lambda i, j: (i, 0, j)),
        out_shape=jax.ShapeDtypeStruct((DEPTH, rows, n), F32),
        compiler_params=_cparams(("arbitrary", "arbitrary")),
        name="ada_mod",
    )(s_all, ada_w, ada_b.reshape(DEPTH, 1, n))


def _rope(y, cos, sin_signed):
    lane = lax.broadcasted_iota(jnp.int32, (1, LANES), 1)
    first_half = (lane % (2 * ROPE_FREQS)) < ROPE_FREQS
    outs = []
    for j in range(y.shape[1] // LANES):
        yj = y[:, j * LANES:(j + 1) * LANES]
        up = pltpu.roll(yj, LANES - ROPE_FREQS, 1)
        dn = pltpu.roll(yj, ROPE_FREQS, 1)
        partner = jnp.where(first_half, up, dn)
        outs.append(yj * cos + partner * sin_signed)
    return jnp.concatenate(outs, axis=1)


def _piece_transpose(v):
    piece = lax.broadcasted_iota(jnp.int32, (1, LANES), 1) // SSM_GROUP
    v = list(v)
    for dist in (4, 2, 1):
        keep = (piece & dist) == 0
        nv = list(v)
        for i in range(PIECES):
            if i & dist == 0:
                a, b = v[i], v[i + dist]
                nv[i] = jnp.where(keep, a, pltpu.roll(b, dist * SSM_GROUP, 1))
                nv[i + dist] = jnp.where(keep, pltpu.roll(a, LANES - dist * SSM_GROUP, 1), b)
        v = nv
    return v


def _norm_mod(x_ref, mod_ref, g_ref):
    x = x_ref[...]
    r = lax.rsqrt(jnp.mean(x * x, axis=-1, keepdims=True) + EPS)
    shift = mod_ref[:, 0:1, :]
    scale = mod_ref[:, 1:2, :]
    return (x * r * g_ref[...]) * (1.0 + scale) + shift


EV_Q, EV_K, EV_V, EV_GA, EV_U, EV_GS = 0, 512, 768, 1024, 1536, 2048
EV_N = 2560


def _even_in_kernel(*refs, use_rope):
    if use_rope:
        x_ref, mod_ref, g_ref, w_ref, cos_ref, sin_ref = refs[:6]
        refs = refs[6:]
    else:
        x_ref, mod_ref, g_ref, w_ref = refs[:4]
        refs = refs[4:]
    q_ref, k_ref, v_ref, ga_ref, z_ref, gs_ref, u_scr = refs
    nb, tt, d = x_ref.shape
    rows = nb * tt
    ab = _norm_mod(x_ref, mod_ref, g_ref).reshape(rows, d).astype(BF16)

    def proj(start, width):
        return jnp.dot(ab, w_ref[:, start:start + width], preferred_element_type=F32)

    def put(o_ref, y):
        o_ref[...] = y.astype(o_ref.dtype).reshape(o_ref.shape)

    u = proj(EV_U, 512)
    for j in range(SSM_WIDTH // LANES):
        u_scr[j] = u[:, j * LANES:(j + 1) * LANES]
    q = proj(EV_Q, 512)
    k = proj(EV_K, 256)
    for kk in range(tt // CHUNK):
        for j in range(SSM_WIDTH // LANES):
            for hh in range(CHUNK // PIECES):
                w = _piece_transpose(
                    [pltpu.bitcast(
                        u_scr[j, pl.ds(kk * CHUNK + hh * PIECES + i, nb, stride=tt), :].astype(BF16),
                        jnp.uint32) for i in range(PIECES)])
                for qi in range(PIECES):
                    g = j * PIECES + qi
                    col = g * GFLAT + hh * LANES
                    z_ref[kk * nb:(kk + 1) * nb, col:col + LANES] = pltpu.bitcast(w[qi], BF16)
    if use_rope:
        cos = jnp.concatenate([cos_ref[...]] * nb, axis=0)
        sin = jnp.concatenate([sin_ref[...]] * nb, axis=0)
        q = _rope(q, cos, sin)
        k = _rope(k, cos, sin)
    put(q_ref, q * (HEAD_DIM ** -0.5 * LOG2E))
    put(k_ref, k)
    put(v_ref, proj(EV_V, 256))
    put(ga_ref, proj(EV_GA, 512))
    put(gs_ref, proj(EV_GS, 512))


def _even_in_call(x, mods, li, mrow, g_all, w_all, lj, rope_tabs=None, name="even_in"):
    b, t, d = x.shape
    tt = TOK_BLK
    use_rope = rope_tabs is not None
    bm = b if mrow == 0 else 1
    tok = lambda wd: pl.BlockSpec((b, tt, wd), lambda ti: (0, ti, 0))
    in_specs = [
        tok(d),
        pl.BlockSpec((None, bm, 3, d), lambda ti: (li, mrow // bm, 0, 0)),
        pl.BlockSpec((None, 1, d), lambda ti: (li, 0, 0)),
        pl.BlockSpec((None, d, EV_N), lambda ti: (lj, 0, 0)),
    ]
    args = [x, mods, g_all, w_all]
    if use_rope:
        in_specs += [pl.BlockSpec((tt, LANES), lambda ti: (ti, 0))] * 2
        args += list(rope_tabs)
    flat_rows = (tt // CHUNK) * b
    widths = (512, 256, 256, 512, None, 512)
    out_specs, out_shape = [], []
    for wd in widths:
        if wd is None:
            out_specs.append(pl.BlockSpec((flat_rows, FLAT_W), lambda ti: (ti, 0)))
            out_shape.append(jax.ShapeDtypeStruct(((t // CHUNK) * b, FLAT_W), BF16))
        else:
            out_specs.append(tok(wd))
            out_shape.append(jax.ShapeDtypeStruct((b, t, wd), BF16))
    return pl.pallas_call(
        functools.partial(_even_in_kernel, use_rope=use_rope),
        grid=(t // tt,),
        in_specs=in_specs,
        out_specs=out_specs,
        out_shape=out_shape,
        scratch_shapes=[pltpu.VMEM((SSM_WIDTH // LANES, b * tt, LANES), F32)],
        compiler_params=_cparams(("parallel",)),
        name=name,
    )(*args)


def _attn_scores(q_ref, kc_ref, k_ref, bias_ref, r0, q0, seq_len, heads=tuple(range(KV_HEADS))):
    pair_w = ATTN_GROUP * HEAD_DIM
    head_of_lane = lax.broadcasted_iota(jnp.int32, (1, pair_w), 1) // HEAD_DIM
    nt_dims = (((1,), (1,)), ((), ()))
    local = k_ref is not None
    if local:
        start = pl.multiple_of(jnp.clip(q0 - BLOCK, 0, seq_len - 3 * BLOCK), BLOCK)
        variant = jnp.where(q0 == 0, 0, jnp.where(q0 == seq_len - BLOCK, 2, 1))
        bias1 = bias_ref[variant]
        bias = jnp.concatenate([bias1] * ATTN_GROUP, axis=0)
    out = []
    for hk in heads:
        lanes = slice(hk * pair_w, (hk + 1) * pair_w)
        kv_lanes = slice(hk * 2 * HEAD_DIM, (hk + 1) * 2 * HEAD_DIM)
        qb = q_ref[0, pl.ds(r0, BLOCK), lanes]
        zero = jnp.zeros_like(qb)
        qs = jnp.concatenate(
            [jnp.where(head_of_lane == g, qb, zero) for g in range(ATTN_GROUP)], axis=0)
        kc = kc_ref[0, :, kv_lanes]
        kc4 = jnp.concatenate([kc, kc], axis=1)
        s_ctx = lax.dot_general(qs, kc4, nt_dims, preferred_element_type=F32)
        if local:
            kl = k_ref[0, pl.ds(start, 3 * BLOCK), kv_lanes]
            kl4 = jnp.concatenate([kl, kl], axis=1)
            s_loc = lax.dot_general(qs, kl4, nt_dims, preferred_element_type=F32) + bias
            out.append((s_ctx, s_loc))
        else:
            out.append((s_ctx,))
    return out


def _attn_finish(scores, sink_ref, lj, vc_ref, v_ref, ga_ref, o_ref, r0, q0, seq_len,
                 heads=tuple(range(KV_HEADS))):
    _attn_finish_items([(r0, q0, hk, scores[n]) for n, hk in enumerate(heads)],
                       sink_ref, lj, vc_ref, v_ref, ga_ref, o_ref, seq_len)


def _attn_finish_items(items, sink_ref, lj, vc_ref, v_ref, ga_ref, o_ref, seq_len):
    pair_w = ATTN_GROUP * HEAD_DIM
    head_of_lane = lax.broadcasted_iota(jnp.int32, (1, pair_w), 1) // HEAD_DIM
    local = v_ref is not None
    st = []
    for r0, q0, hk, parts in items:
        tiles = [p[:, t * LANES:(t + 1) * LANES] for p in parts for t in range(p.shape[1] // LANES)]
        sink_col = jnp.concatenate(
            [jnp.full((BLOCK, 1), sink_ref[lj, hk * ATTN_GROUP + g] * LOG2E, F32)
             for g in range(ATTN_GROUP)], axis=0)
        mt = tiles[0]
        for tl in tiles[1:]:
            mt = jnp.maximum(mt, tl)
        m = jnp.maximum(jnp.max(mt, axis=-1, keepdims=True), sink_col)
        st.append((sink_col, m))
    es_all = []
    for (r0, q0, hk, parts), (sink_col, m) in zip(items, st):
        es = [jnp.exp2(p - m) for p in parts]
        et = None
        for e in es:
            for t in range(e.shape[1] // LANES):
                tl = e[:, t * LANES:(t + 1) * LANES]
                et = tl if et is None else et + tl
        den = jnp.sum(et, axis=-1, keepdims=True) + jnp.exp2(sink_col - m)
        es_all.append(([e.astype(BF16) for e in es], den))
    outs = []
    for (r0, q0, hk, parts), (es, den) in zip(items, es_all):
        kv_lanes = slice(hk * 2 * HEAD_DIM, (hk + 1) * 2 * HEAD_DIM)
        vc = vc_ref[0, :, kv_lanes]
        vc4 = jnp.concatenate([vc, vc], axis=1)
        o = jnp.dot(es[0], vc4, preferred_element_type=F32)
        if local:
            start = pl.multiple_of(jnp.clip(q0 - BLOCK, 0, seq_len - 3 * BLOCK), BLOCK)
            vl = v_ref[0, pl.ds(start, 3 * BLOCK), kv_lanes]
            vl4 = jnp.concatenate([vl, vl], axis=1)
            o = o + jnp.dot(es[1], vl4, preferred_element_type=F32)
        outs.append(o * (1.0 / den))
    for (r0, q0, hk, parts), o in zip(items, outs):
        lanes = slice(hk * pair_w, (hk + 1) * pair_w)
        oh = jnp.zeros((BLOCK, pair_w), F32)
        for g in range(ATTN_GROUP):
            oh = oh + jnp.where(head_of_lane == g, o[g * BLOCK:(g + 1) * BLOCK, :], 0.0)
        gate = ga_ref[0, pl.ds(r0, BLOCK), lanes].astype(F32)
        o_ref[0, pl.ds(r0, BLOCK), lanes] = (oh * _silu(gate)).astype(o_ref.dtype)


def _attn_kernel(sink_ref, q_ref, kc_ref, vc_ref, ga_ref, bias_ref, k_ref, v_ref, o_ref,
                 sc_scr, sl_scr, *, tq, seq_len, lj):
    qi = pl.program_id(1)
    nblk = tq // BLOCK

    def scores_to(i, slot, heads):
        r0 = pl.multiple_of(i * BLOCK, BLOCK)
        sc = _attn_scores(q_ref, kc_ref, k_ref, bias_ref, r0, qi * tq + i * BLOCK, seq_len, heads)
        for n, hk in enumerate(heads):
            sc_scr[slot, hk] = sc[n][0]
            sl_scr[slot, hk] = sc[n][1]

    def finish_from(i, slot, heads):
        r0 = pl.multiple_of(i * BLOCK, BLOCK)
        sc = [(sc_scr[slot, hk], sl_scr[slot, hk]) for hk in heads]
        _attn_finish(sc, sink_ref, lj, vc_ref, v_ref, ga_ref, o_ref, r0, qi * tq + i * BLOCK, seq_len,
                     heads)

    both = tuple(range(KV_HEADS))
    scores_to(0, 0, both)

    def body(j, carry):
        i = 2 * j
        for hk in both:
            scores_to(i + 1, 1, (hk,))
            finish_from(i, 0, (hk,))
        for hk in both:
            scores_to(i + 2, 0, (hk,))
            finish_from(i + 1, 1, (hk,))
        return carry

    lax.fori_loop(0, nblk // 2 - 1, body, 0)
    for hk in both:
        scores_to(nblk - 1, 1, (hk,))
        finish_from(nblk - 2, 0, (hk,))
    for hk in both:
        finish_from(nblk - 1, 1, (hk,))


def _attn_ctx_kernel(sink_ref, q_ref, kc_ref, vc_ref, ga_ref, o_ref, *, tq, lj):
    items = []
    for i in range(tq // BLOCK):
        sc = _attn_scores(q_ref, kc_ref, None, None, i * BLOCK, 0, 0)
        items += [(i * BLOCK, 0, hk, sc[hk]) for hk in range(KV_HEADS)]
    _attn_finish_items(items, sink_ref, lj, vc_ref, None, ga_ref, o_ref, 0)


def _band_bias():
    i = np.arange(BLOCK)[:, None]
    j = np.arange(3 * BLOCK)[None, :]
    tabs = []
    for off in (0, -BLOCK, -2 * BLOCK):
        valid = np.abs(i - (j + off)) <= WINDOW
        tabs.append(np.where(valid, 0.0, NEG_INF))
    return jnp.asarray(np.stack(tabs), F32)


def _attn_call(sink, lj, q, kc4, vc4, ga, k4=None, v4=None, tq=ATTN_TQ, name="attn"):
    b, t, _ = q.shape
    lc = kc4.shape[1]
    local = k4 is not None
    tq = tq if local else t
    in_specs = [
        pl.BlockSpec(memory_space=pltpu.SMEM),
        pl.BlockSpec((1, tq, ATTN_WIDTH), lambda bi, ti: (bi, ti, 0)),
        pl.BlockSpec((1, lc, 2 * KV_WIDTH), lambda bi, ti: (bi, 0, 0)),
        pl.BlockSpec((1, lc, 2 * KV_WIDTH), lambda bi, ti: (bi, 0, 0)),
        pl.BlockSpec((1, tq, ATTN_WIDTH), lambda bi, ti: (bi, ti, 0)),
    ]
    args = [sink, q, kc4, vc4, ga]
    scratch = []
    if local:
        in_specs += [pl.BlockSpec((3, BLOCK, 3 * BLOCK), lambda bi, ti: (0, 0, 0))]
        in_specs += [pl.BlockSpec((1, t, 2 * KV_WIDTH), lambda bi, ti: (bi, 0, 0))] * 2
        args += [_band_bias(), k4, v4]
        rows = ATTN_GROUP * BLOCK
        scratch = [pltpu.VMEM((2, KV_HEADS, rows, lc), F32),
                   pltpu.VMEM((2, KV_HEADS, rows, 3 * BLOCK), F32)]
        body = functools.partial(_attn_kernel, tq=tq, seq_len=t, lj=lj)
    else:
        body = functools.partial(_attn_ctx_kernel, tq=tq, lj=lj)
    return pl.pallas_call(
        body,
        grid=(b, t // tq),
        in_specs=in_specs,
        out_specs=pl.BlockSpec((1, tq, ATTN_WIDTH), lambda bi, ti: (bi, ti, 0)),
        out_shape=jax.ShapeDtypeStruct((b, t, ATTN_WIDTH), BF16),
        scratch_shapes=scratch,
        compiler_params=_cparams(("parallel", "arbitrary")),
        name=name,
    )(*args)


S5_TILE = 512
CROWS = 16


def _s5_kernel(zc_ref, zl_ref, m_ref, s_ref, o_ref, a_ref, d_ref, yc_ref, yl_ref,
               hc_buf, hl_buf, sc_buf, sl_buf):
    half = 2 * SSM_STATE
    nc_rows = zc_ref.shape[0]
    nl_rows = zl_ref.shape[0]

    sc_buf[...] = jnp.dot(zc_ref[...], s_ref[0], preferred_element_type=F32)
    for t in range(nl_rows // S5_TILE):
        rows = pl.ds(t * S5_TILE, S5_TILE)
        sl_buf[rows, :] = jnp.dot(zl_ref[rows, :], s_ref[0], preferred_element_type=F32)

    a_f = (a_ref[0, 0:1, :], a_ref[0, 1:2, :])
    a_b = (a_ref[0, 2:3, :], a_ref[0, 3:4, :])

    def scan(s_buf, h_buf, nrows, carry):
        nchunks = nrows // CROWS

        def one(a, h, s_buf, h_buf, row, lane0):
            h_re, h_im = h
            h_buf[pl.ds(row, CROWS), lane0:lane0 + 2 * half] = (
                jnp.concatenate([h_re, h_im], axis=1).astype(BF16))
            s_re = s_buf[pl.ds(row, CROWS), lane0:lane0 + half]
            s_im = s_buf[pl.ds(row, CROWS), lane0 + half:lane0 + 2 * half]
            return (a[0] * h_re - a[1] * h_im + s_re, a[0] * h_im + a[1] * h_re + s_im)

        def step(i, c):
            hf, hb = c
            rf = pl.multiple_of(i * CROWS, CROWS)
            rb = pl.multiple_of((nchunks - 1 - i) * CROWS, CROWS)
            return (one(a_f, hf, s_buf, h_buf, rf, 0), one(a_b, hb, s_buf, h_buf, rb, 2 * half))

        return lax.fori_loop(0, nchunks, step, carry, unroll=2)

    zero = jnp.zeros((CROWS, half), F32)
    carry = scan(sc_buf, hc_buf, nc_rows, ((zero, zero), (zero, zero)))
    scan(sl_buf, hl_buf, nl_rows, carry)

    def emit(z_ref, h_buf, y_ref, row0, nrows):
        z = z_ref[pl.ds(row0, nrows), :]
        y = jnp.concatenate(
            [jnp.dot(z[:, gi * GFLAT:(gi + 1) * GFLAT], m_ref[0, gi], preferred_element_type=F32)
             for gi in range(2)], axis=1)
        y = y + jnp.dot(h_buf[pl.ds(row0, nrows), :], o_ref[0], preferred_element_type=F32)
        y = y + z.astype(F32) * d_ref[0]
        y_ref[pl.ds(row0, nrows), :] = y.astype(y_ref.dtype)

    emit(zc_ref, hc_buf, yc_ref, 0, nc_rows)
    for t in range(nl_rows // S5_TILE):
        emit(zl_ref, hl_buf, yl_ref, t * S5_TILE, S5_TILE)


def _s5_call(zc, zl, m_op, s_op, o_op, a_op, d_op, lj):
    rc, rl = zc.shape[0], zl.shape[0]
    pw = 2 * GFLAT
    row_spec = lambda r: pl.BlockSpec((r, pw), lambda p: (0, p))
    op_spec = pl.BlockSpec((None, 1, pw, pw), lambda p: (lj, p, 0, 0))
    return pl.pallas_call(
        _s5_kernel,
        grid=(PAIRS,),
        in_specs=[
            row_spec(rc), row_spec(rl),
            pl.BlockSpec((None, 1, 2, GFLAT, GFLAT), lambda p: (lj, p, 0, 0, 0)), op_spec, op_spec,
            pl.BlockSpec((None, 1, 4, 2 * SSM_STATE), lambda p: (lj, p, 0, 0)),
            pl.BlockSpec((None, 1, 1, pw), lambda p: (lj, p, 0, 0)),
        ],
        out_specs=[row_spec(rc), row_spec(rl)],
        out_shape=[jax.ShapeDtypeStruct(zc.shape, YFLAT_DTYPE), jax.ShapeDtypeStruct(zl.shape, YFLAT_DTYPE)],
        scratch_shapes=[
            pltpu.VMEM((rc, pw), BF16),
            pltpu.VMEM((rl, pw), BF16),
            pltpu.VMEM((rc, pw), F32),
            pltpu.VMEM((rl, pw), F32),
        ],
        compiler_params=_cparams(("parallel",)),
        name="s5_scan",
    )(zc, zl, m_op, s_op, o_op, a_op, d_op)


def _s5_op_kernel(lr_ref, li_ref, bbr_ref, bbi_ref, ccr_ref, cci_ref, m_ref, s_ref, o_ref):
    t = CHUNK
    hp = lax.Precision.HIGHEST
    nt_dims = (((1,), (1,)), ((), ()))
    kk = lax.broadcasted_iota(jnp.int32, (t, 1), 0).astype(F32)
    s_idx = lax.broadcasted_iota(jnp.int32, (GFLAT, GFLAT), 0) // SSM_GROUP
    t_idx = lax.broadcasted_iota(jnp.int32, (GFLAT, GFLAT), 1) // SSM_GROUP
    half = 2 * SSM_STATE

    def rep_k(v):
        return jnp.concatenate(
            [jnp.broadcast_to(v[k:k + 1, :], (SSM_GROUP, SSM_STATE)) for k in range(t)], axis=0)

    def rep_c(v):
        return jnp.concatenate([v] * t, axis=0)

    s_ref[...] = jnp.zeros(s_ref.shape, s_ref.dtype)
    o_ref[...] = jnp.zeros(o_ref.shape, o_ref.dtype)
    m_acc = [None, None]
    for q in range(4):
        direction, gi = q // 2, q % 2
        lr, li = lr_ref[q], li_ref[q]
        bb = (bbr_ref[q], bbi_ref[q])
        cc = (ccr_ref[q], cci_ref[q])

        def rows_of(expo, v):
            mag = jnp.exp(lr * expo)
            pr, pi = mag * jnp.cos(li * expo), mag * jnp.sin(li * expo)
            pr, pi = rep_k(pr), rep_k(pi)
            vr, vi = rep_c(v[0]), rep_c(v[1])
            return pr * vr - pi * vi, pr * vi + pi * vr

        if direction == 0:
            e_mr, e_mc, e_s, e_o = -kk, kk, (t - 1.0) - kk, kk + 1.0
            keep = t_idx >= s_idx
        else:
            e_mr, e_mc, e_s, e_o = kk, -kk, kk, t - kk
            keep = s_idx >= t_idx
        rr, ri = rows_of(e_mr, bb)
        cr, ci = rows_of(e_mc, cc)
        mq = lax.dot_general(jnp.concatenate([rr, -ri], axis=1), jnp.concatenate([cr, ci], axis=1),
                             nt_dims, precision=hp, preferred_element_type=F32)
        mq = jnp.where(keep, mq, 0.0)
        m_acc[gi] = mq if m_acc[gi] is None else m_acc[gi] + mq
        sr, si = rows_of(e_s, bb)
        rows = slice(gi * GFLAT, (gi + 1) * GFLAT)
        c0 = direction * 2 * half + gi * SSM_STATE
        s_ref[rows, c0:c0 + SSM_STATE] = sr.astype(s_ref.dtype)
        s_ref[rows, c0 + half:c0 + half + SSM_STATE] = si.astype(s_ref.dtype)
        orr, ori = rows_of(e_o, cc)
        o_t, oi_t = orr.T, (-ori).T
        o_ref[c0:c0 + SSM_STATE, rows] = o_t.astype(o_ref.dtype)
        o_ref[c0 + half:c0 + half + SSM_STATE, rows] = oi_t.astype(o_ref.dtype)
    for gi in range(2):
        m_ref[gi] = m_acc[gi].astype(m_ref.dtype)


def _s5_operators(a_re, a_im, log_dt, b_re, b_im, c_re, c_im, d_skip):
    t = CHUNK
    ne = a_re.shape[0]
    ar, ai = a_re.astype(F32), a_im.astype(F32)
    dt = jnp.exp(log_dt.astype(F32))[..., None]
    lr, li = ar * dt, ai * dt
    mag = jnp.exp(lr)
    xr, xi = mag * jnp.cos(li) - 1.0, mag * jnp.sin(li)
    den = ar * ar + ai * ai
    cr, ci = (xr * ar + xi * ai) / den, (xi * ar - xr * ai) / den
    br = jnp.swapaxes(b_re.astype(F32), -1, -2)
    bi = jnp.swapaxes(b_im.astype(F32), -1, -2)
    bbr = cr[..., None, :] * br - ci[..., None, :] * bi
    bbi = cr[..., None, :] * bi + ci[..., None, :] * br

    def per_pair(v):
        tail = v.shape[3:]
        v = v.reshape((ne, 2, PAIRS, 2) + tail)
        v = jnp.moveaxis(v, 1, 2)
        return v.reshape((ne, PAIRS, 4) + tail)

    small = [per_pair(v[..., None, :]) for v in (lr, li)]
    mats = [per_pair(v) for v in (bbr, bbi, c_re.astype(F32), c_im.astype(F32))]
    spec = lambda rws: pl.BlockSpec((None, None, 4, rws, SSM_STATE), lambda n, p: (n, p, 0, 0, 0))
    pw = 2 * GFLAT
    m_pair, s_pair, o_pair = pl.pallas_call(
        _s5_op_kernel,
        grid=(ne, PAIRS),
        in_specs=[spec(1), spec(1)] + [spec(SSM_GROUP)] * 4,
        out_specs=[pl.BlockSpec((None, None, 2, GFLAT, GFLAT), lambda n, p: (n, p, 0, 0, 0)),
                   pl.BlockSpec((None, None, pw, pw), lambda n, p: (n, p, 0, 0)),
                   pl.BlockSpec((None, None, pw, pw), lambda n, p: (n, p, 0, 0))],
        out_shape=[jax.ShapeDtypeStruct((ne, PAIRS, 2, GFLAT, GFLAT), BF16),
                   jax.ShapeDtypeStruct((ne, PAIRS, pw, pw), BF16),
                   jax.ShapeDtypeStruct((ne, PAIRS, pw, pw), BF16)],
        compiler_params=_cparams(("parallel", "parallel")),
        name="s5_ops",
    )(*small, *mats)

    magt = jnp.exp(lr * t)
    atr, ati = magt * jnp.cos(li * t), magt * jnp.sin(li * t)
    a_pair = jnp.stack([v[:, d].reshape(ne, PAIRS, 2 * SSM_STATE)
                        for d in range(2) for v in (atr, ati)], axis=2)
    d_g = d_skip.astype(F32).reshape(ne, PAIRS, 2, 1, SSM_GROUP)
    d_flat = jnp.broadcast_to(d_g, (ne, PAIRS, 2, CHUNK, SSM_GROUP)).reshape(ne, PAIRS, 1, 2 * GFLAT)
    return m_pair, s_pair, o_pair, a_pair, d_flat


def _even_out_kernel(og_ref, yf_ref, gs_ref, h_ref, mod_ref, gw_ref, gb_ref, wo_ref, o_ref, y_scr,
                     acc_scr):
    nb, tt, d = h_ref.shape
    rows = nb * tt
    og = og_ref[...].reshape(rows, ATTN_WIDTH)
    acc_scr[...] = jnp.dot(og, wo_ref[0:ATTN_WIDTH, :], preferred_element_type=F32)
    for kk in range(tt // CHUNK):
        for j in range(SSM_WIDTH // LANES):
            for hh in range(CHUNK // PIECES):
                v = []
                for qi in range(PIECES):
                    col = (j * PIECES + qi) * GFLAT + hh * LANES
                    v.append(pltpu.bitcast(yf_ref[kk * nb:(kk + 1) * nb, col:col + LANES], jnp.uint32))
                w = _piece_transpose(v)
                for i in range(PIECES):
                    tok = kk * CHUNK + hh * PIECES + i
                    y_scr[j, pl.ds(tok, nb, stride=tt), :] = pltpu.bitcast(w[i], BF16).astype(F32)
    y = jnp.concatenate([y_scr[j] for j in range(SSM_WIDTH // LANES)], axis=1)
    z = _gelu_tanh(y)
    t = jnp.dot(z.astype(BF16), gw_ref[...], preferred_element_type=F32) + gb_ref[...]
    gs = gs_ref[...].reshape(rows, SSM_WIDTH).astype(F32)
    os_ = z * _sigmoid(t) * _silu(gs)
    yo = acc_scr[...] + jnp.dot(os_.astype(BF16), wo_ref[ATTN_WIDTH:, :], preferred_element_type=F32)
    o_ref[...] = h_ref[...] + mod_ref[:, 2:3, :] * yo.reshape(nb, tt, d)


def _even_out_call(og, yflat, gs, h, mods, li, mrow, gw_all, gb_all, wo_all, lj, name="even_out"):
    b, t, d = h.shape
    tt = TOK_BLK
    bm = b if mrow == 0 else 1
    tok = lambda w: pl.BlockSpec((b, tt, w), lambda ti: (0, ti, 0))
    layer = lambda r, c: pl.BlockSpec((None, r, c), lambda ti: (lj, 0, 0))
    return pl.pallas_call(
        _even_out_kernel,
        grid=(t // tt,),
        in_specs=[tok(ATTN_WIDTH),
                  pl.BlockSpec(((tt // CHUNK) * b, FLAT_W), lambda ti: (ti, 0)),
                  tok(SSM_WIDTH), tok(d),
                  pl.BlockSpec((None, bm, 3, d), lambda ti: (li, mrow // bm, 0, 0)),
                  layer(SSM_WIDTH, SSM_WIDTH), layer(1, SSM_WIDTH), layer(d, d)],
        out_specs=tok(d),
        out_shape=jax.ShapeDtypeStruct((b, t, d), F32),
        scratch_shapes=[pltpu.VMEM((SSM_WIDTH // LANES, b * tt, LANES), F32),
                        pltpu.VMEM((b * tt, d), F32)],
        compiler_params=_cparams(("parallel",)),
        name=name,
    )(og, yflat, gs, h, mods, gw_all, gb_all, wo_all)


def _window_sum(x, r):
    n = x.shape[0]
    up = lambda v, k: pltpu.roll(v, n - k, 0)
    acc, span = x, 1
    while span * 2 <= r * 2:
        acc = acc + up(acc, span)
        span *= 2
    return pltpu.roll(acc, r, 0) + up(x, r)


def _odd_kernel(x_ref, xp_ref, xn_ref, mod_ref, g_ref, wi_ref, pw_ref, ps_ref, wo_ref, fg_ref,
                o_ref, *, tm, seq_len, final):
    ti = pl.program_id(1)
    nt = pl.num_programs(1)
    d = x_ref.shape[-1]
    x = x_ref[0]
    x_ext = jnp.concatenate([xp_ref[0], x, xn_ref[0]], axis=0)
    r = lax.rsqrt(jnp.mean(x_ext * x_ext, axis=-1, keepdims=True) + EPS)
    a_ext = (x_ext * r * g_ref[...]) * (1.0 + mod_ref[0, 1:2, :]) + mod_ref[0, 0:1, :]
    ab = a_ext.astype(BF16)
    u_ext = jnp.dot(ab, wi_ref[:, 0:d], preferred_element_type=F32)
    row = lax.broadcasted_iota(jnp.int32, (tm + 2 * POOL_HALO, 1), 0)
    inside = jnp.logical_and(jnp.logical_or(ti > 0, row >= POOL_HALO),
                             jnp.logical_or(ti < nt - 1, row < POOL_HALO + tm))
    u_ext = jnp.where(inside, u_ext, 0.0)
    gate = jnp.dot(ab[POOL_HALO:POOL_HALO + tm], wi_ref[:, d:2 * d], preferred_element_type=F32)
    u = u_ext[POOL_HALO:POOL_HALO + tm]
    pos = ti * tm + lax.broadcasted_iota(jnp.int32, (tm, 1), 0)
    parts = []
    for gi, w in enumerate(POOL_WINDOWS):
        rad = w // 2
        lanes = slice(gi * POOL_GROUP, (gi + 1) * POOL_GROUP)
        ws = _window_sum(u_ext[:, lanes], rad)[POOL_HALO:POOL_HALO + tm]
        cnt = (jnp.minimum(pos + rad + 1, seq_len) - jnp.maximum(pos - rad, 0)).astype(F32)
        p = ws * (1.0 / cnt) - u[:, lanes]
        parts.append(jnp.dot(p.astype(BF16), pw_ref[gi], preferred_element_type=F32))
    p_all = jnp.concatenate(parts, axis=1) * ps_ref[...]
    mixed = p_all * _silu(gate)
    yo = jnp.dot(mixed.astype(BF16), wo_ref[...], preferred_element_type=F32)
    hn = x + mod_ref[0, 2:3, :] * yo
    if final:
        rr = lax.rsqrt(jnp.mean(hn * hn, axis=-1, keepdims=True) + EPS)
        hn = hn * rr * fg_ref[...]
    o_ref[0] = hn


def _odd_call(h, mods, li, mrow, g_all, w_in_all, pool_w_all, pool_scale_all, wo_all, lj, final_g,
              tm, final, name="odd"):
    b, t, d = h.shape
    hb = tm // POOL_HALO
    nhalo = t // POOL_HALO
    mod_map = (lambda bi, ti: (li, bi, 0, 0)) if mrow == 0 else (lambda bi, ti: (li, mrow, 0, 0))
    tok = pl.BlockSpec((1, tm, d), lambda bi, ti: (bi, ti, 0))
    npw = len(POOL_WINDOWS)
    return pl.pallas_call(
        functools.partial(_odd_kernel, tm=tm, seq_len=t, final=final),
        grid=(b, t // tm),
        in_specs=[
            tok,
            pl.BlockSpec((1, POOL_HALO, d), lambda bi, ti: (bi, jnp.maximum(ti * hb - 1, 0), 0)),
            pl.BlockSpec((1, POOL_HALO, d), lambda bi, ti: (bi, jnp.minimum((ti + 1) * hb, nhalo - 1), 0)),
            pl.BlockSpec((None, 1, 3, d), mod_map),
            pl.BlockSpec((None, 1, d), lambda bi, ti: (li, 0, 0)),
            pl.BlockSpec((None, d, 2 * d), lambda bi, ti: (lj, 0, 0)),
            pl.BlockSpec((None, npw, POOL_GROUP, POOL_GROUP), lambda bi, ti: (lj, 0, 0, 0)),
            pl.BlockSpec((None, 1, d), lambda bi, ti: (lj, 0, 0)),
            pl.BlockSpec((None, d, d), lambda bi, ti: (lj, 0, 0)),
            pl.BlockSpec((1, d), lambda bi, ti: (0, 0)),
        ],
        out_specs=tok,
        out_shape=jax.ShapeDtypeStruct((b, t, d), F32),
        compiler_params=_cparams(("parallel", "arbitrary")),
        name=name,
    )(h, h, h, mods, g_all, w_in_all, pool_w_all, pool_scale_all, wo_all, final_g)


def _rope_tables(n_tokens):
    pos = np.arange(n_tokens)
    inv_freq = ROPE_BASE ** (-np.arange(ROPE_FREQS, dtype=np.float64) / ROPE_FREQS)
    ang_r = (pos // GRID_W)[:, None] * inv_freq
    ang_c = (pos % GRID_W)[:, None] * inv_freq
    cos_h = np.concatenate([np.cos(ang_r)] * 2 + [np.cos(ang_c)] * 2, axis=1)
    sin_h = np.concatenate([-np.sin(ang_r), np.sin(ang_r), -np.sin(ang_c), np.sin(ang_c)], axis=1)
    return (jnp.asarray(np.tile(cos_h, (1, 2)), F32), jnp.asarray(np.tile(sin_h, (1, 2)), F32))


def _even_weights(w_in):
    c = np.cumsum([0, ATTN_WIDTH, KV_WIDTH, KV_WIDTH, ATTN_WIDTH, SSM_WIDTH, SSM_WIDTH])
    wq = w_in[..., c[0]:c[1]]
    wk = w_in[..., c[1]:c[2]]
    wv = w_in[..., c[2]:c[3]]
    dup = lambda w: jnp.concatenate([w[..., :HEAD_DIM]] * 2 + [w[..., HEAD_DIM:]] * 2, axis=-1)
    w = jnp.concatenate([wq, dup(wk), dup(wv), w_in[..., c[3]:]], axis=-1)
    return w.astype(BF16)


def kernel(x, c, ctx, c_ctx, ada_w, ada_b, norm_g, even_w_in, even_w_out, attn_sink,
           ssm_a_re, ssm_a_im, ssm_log_dt, ssm_b_re, ssm_b_im, ssm_c_re, ssm_c_im, ssm_d,
           glu_w, glu_b, odd_w_in, odd_w_out, pool_w, pool_scale, final_g):
    b, l, d = x.shape
    lc = ctx.shape[1]
    tm = 512
    rope_tabs = _rope_tables(l)

    s_all = jnp.concatenate([c, c_ctx[None, :], jnp.zeros((24 - b - 1, d), F32)], axis=0)
    mods = _ada_call(s_all, ada_w, ada_b).reshape(DEPTH, 24, 3, d)
    s5_ops = _s5_operators(ssm_a_re, ssm_a_im, ssm_log_dt, ssm_b_re, ssm_b_im,
                           ssm_c_re, ssm_c_im, ssm_d)
    g_all = norm_g.reshape(DEPTH, 1, d)
    ew_in, ew_out, gw_all = _even_weights(even_w_in), even_w_out.astype(BF16), glu_w.astype(BF16)
    gb_all = glu_b.reshape(-1, 1, SSM_WIDTH)
    ow_in, ow_out, pw_all = odd_w_in.astype(BF16), odd_w_out.astype(BF16), pool_w.astype(BF16)
    ps_all = pool_scale.reshape(-1, 1, d)
    fg = final_g.reshape(1, d)
    sink = attn_sink.astype(F32)
    h, hc = x, ctx
    for i in range(DEPTH):
        need_ctx = i < DEPTH - 1
        j = i // 2
        if i % 2 == 0:
            q, k4, v4, ga, zl, gs = _even_in_call(h, mods, i, 0, g_all, ew_in, j, rope_tabs, name="even_in")
            qc, kc4, vc4, gac, zc, gsc = _even_in_call(hc, mods, i, b, g_all, ew_in, j, None,
                                                       name="even_in_ctx")
            og = _attn_call(sink, j, q, kc4, vc4, ga, k4, v4, name="attn")
            yc_f, yl_f = _s5_call(zc, zl, *s5_ops, j)
            h_new = _even_out_call(og, yl_f, gs, h, mods, i, 0, gw_all, gb_all, ew_out, j, name="even_out")
            if need_ctx:
                ogc = _attn_call(sink, j, qc, kc4, vc4, gac, name="attn_ctx")
                hc = _even_out_call(ogc, yc_f, gsc, hc, mods, i, b, gw_all, gb_all, ew_out, j,
                                    name="even_out_ctx")
            h = h_new
        else:
            final = i == DEPTH - 1
            h_new = _odd_call(h, mods, i, 0, g_all, ow_in, pw_all, ps_all, ow_out, j, fg, tm, final,
                              name="odd")
            if need_ctx:
                hc = _odd_call(hc, mods, i, b, g_all, ow_in, pw_all, ps_all, ow_out, j, fg, lc, False,
                               name="odd_ctx")
            h = h_new
    return h
```
